```python
import math
import jax, jax.numpy as jnp
from jax import lax
import numpy as np

D_MODEL = 1024
BATCH = 8
SEQ = 2048
DEPTH = 4
DEC_BATCH = 128
DEC_SEQ = 8
PAST_LEN = 8192
PAGE_SIZE = 128

N_A_LAYERS = (DEPTH + 1) // 2
N_C_LAYERS = DEPTH // 2
D_FF = 4 * D_MODEL
EPS = 1e-6
ROPE_THETA = 10000.0
MASK_NEG = -1e30
LB_FLOOR = 1e-30

A_HEADS = D_MODEL // 256
A_QK = 64
A_VD = 2 * A_QK
A_Q_BLOCK = 128
A_QKW = A_HEADS * 2 * A_QK
A_W = A_HEADS * A_VD
B_HEADS = 4
B_DK = D_MODEL // 8
B_DV = D_MODEL // 8
B_CHUNK = 64
B_KW = B_HEADS * B_DK
B_W = B_HEADS * B_DV
C_HEADS = 16
C_HEADDIM = 64
C_DINNER = C_HEADS * C_HEADDIM
C_GROUPS = 2
C_DSTATE = 128
C_CONV = 4
C_CHUNK = 128
C_CONV_DIM = C_DINNER + 2 * C_GROUPS * C_DSTATE
D_HEADS = 8
D_KV_HEADS = 2
D_GROUP = D_HEADS // D_KV_HEADS
D_HD = 64
WINDOW = 128

E_IN = 2 * A_QKW + A_W + 2 * B_KW + 2 * B_W
E_MIX = A_W + B_W
O_IN = C_DINNER + C_CONV_DIM + C_HEADS + D_HEADS * D_HD + 2 * D_KV_HEADS * D_HD
O_MIX = C_DINNER + D_HEADS * D_HD

kernel_name = "hybrid_diffattn_hgrn2_mamba2_swa_step"


def split_cols(x, sizes):
    idx = [int(i) for i in np.cumsum(sizes)[:-1]]
    return jnp.split(x, idx, axis=-1)


def rmsnorm(x, g):
    xf = x.astype(jnp.float32)
    r = xf * lax.rsqrt(jnp.mean(xf * xf, axis=-1, keepdims=True) + EPS)
    return (r * g.astype(jnp.float32)).astype(x.dtype)


def rope(x, pos):
    d = x.shape[-1]
    inv = ROPE_THETA ** (-jnp.arange(0, d, 2, dtype=jnp.float32) / d)
    ang = pos.astype(jnp.float32)[:, None] * inv[None, :]
    cos = jnp.cos(ang)[:, None, :]
    sin = jnp.sin(ang)[:, None, :]
    xf = x.astype(jnp.float32)
    x1, x2 = xf[..., : d // 2], xf[..., d // 2:]
    return jnp.concatenate([x1 * cos - x2 * sin, x2 * cos + x1 * sin], axis=-1).astype(x.dtype)


def to_chunks(a, L):
    b, t = a.shape[:2]
    return jnp.moveaxis(a.reshape((b, t // L, L) + a.shape[2:]), 1, 0)


def from_chunks(a):
    n, b, L = a.shape[:3]
    return jnp.moveaxis(a, 0, 1).reshape((b, n * L) + a.shape[3:])


def diff_lambda(lam, layer):
    lam = lam.astype(jnp.float32)
    lam_init = 0.8 - 0.6 * math.exp(-0.3 * layer)
    lam_full = jnp.exp(jnp.sum(lam[0] * lam[1])) - jnp.exp(jnp.sum(lam[2] * lam[3])) + lam_init
    return lam_full, lam_init


def diff_attend(q, parts, lam):
    scale = A_QK ** -0.5
    scores = []
    for k, _, m in parts:
        s = jnp.einsum("bqhcd,bkhcd->bhcqk", q, k).astype(jnp.float32) * scale
        scores.append(s if m is None else jnp.where(m, s, MASK_NEG))
    p = jax.nn.softmax(jnp.concatenate(scores, axis=-1), axis=-1)
    p = p[:, :, 0] - lam * p[:, :, 1]
    out = None
    off = 0
    for k, v, _ in parts:
        n = k.shape[1]
        o = jnp.einsum("bhqk,bkhd->bqhd", p[..., off:off + n].astype(v.dtype), v)
        out = o if out is None else out + o
        off += n
    return out


def diff_attn_prompt(q, k, v, lam):
    b, t = q.shape[:2]
    nb = t // A_Q_BLOCK
    qb = jnp.moveaxis(q.reshape(b, nb, A_Q_BLOCK, A_HEADS, 2, A_QK), 1, 0)
    k_pos = jnp.arange(t)

    def block(args):
        qi, i = args
        q_pos = i * A_Q_BLOCK + jnp.arange(A_Q_BLOCK)
        mask = k_pos[None, :] <= q_pos[:, None]
        return diff_attend(qi, [(k, v, mask)], lam)

    o = lax.map(block, (qb, jnp.arange(nb)))
    return jnp.moveaxis(o, 0, 1).reshape(b, t, A_HEADS, A_VD)


def diff_attn_sample(q, k, v, k_past, v_past, lam):
    t = q.shape[1]
    causal = jnp.tril(jnp.ones((t, t), bool))
    return diff_attend(q, [(k_past, v_past, None), (k, v, causal)], lam)


def hgrn_lower_bound(p):
    sm = jax.nn.softmax(p.astype(jnp.float32), axis=0)
    return jnp.concatenate([jnp.zeros_like(sm[:1]), jnp.cumsum(sm[1:], axis=0)], axis=0)


def hgrn2_scan(q, f_logit, i, lb, S0):
    f32 = jnp.float32
    t = q.shape[1]
    L = math.gcd(t, B_CHUNK)
    z = f_logit.astype(f32)
    log_lb = jnp.log(jnp.maximum(lb, LB_FLOOR))
    log_f = jnp.logaddexp(log_lb, jnp.log1p(-lb) + jax.nn.log_sigmoid(z))
    k = (1.0 - lb) * jax.nn.sigmoid(-z)
    qf = jax.nn.silu(q.astype(f32))
    vf = i.astype(f32)
    causal = jnp.tril(jnp.ones((L, L), bool))

    def step(S, c):
        qc, kc, vc, gc = c
        G = jnp.cumsum(gc, axis=1)
        o_inter = jnp.einsum("blhk,bhkv->blhv", qc * jnp.exp(G), S)
        rel = jnp.where(causal[None, :, :, None, None], G[:, :, None] - G[:, None], MASK_NEG)
        A = jnp.einsum("bthk,bshk,btshk->bhts", qc, kc, jnp.exp(rel))
        o = o_inter + jnp.einsum("bhts,bshv->bthv", A, vc)
        GL = G[:, -1]
        S = jnp.exp(GL)[..., None] * S + jnp.einsum("bshk,bshv->bhkv", kc * jnp.exp(GL[:, None] - G), vc)
        return S, o

    S, o = lax.scan(step, S0.astype(f32), (to_chunks(qf, L), to_chunks(k, L), to_chunks(vf, L), to_chunks(log_f, L)))
    return from_chunks(o), S


def causal_conv(xbc, conv0, w, b):
    xp = jnp.concatenate([conv0, xbc], axis=1)
    y = lax.conv_general_dilated(xp, w.astype(xp.dtype)[:, None, :], window_strides=(1,), padding="VALID",
                                 dimension_numbers=("NWC", "WIO", "NWC"), feature_group_count=xp.shape[-1])
    return jax.nn.silu(y + b.astype(xp.dtype)), xp[:, -(C_CONV - 1):]


def ssd_scan(x, dt, Bm, Cm, A, S0):
    t = x.shape[1]
    L = math.gcd(t, C_CHUNK)
    a = dt * A
    causal = jnp.tril(jnp.ones((L, L), bool))

    def step(S, c):
        xc, dtc, ac, Bc, Cc = c
        G = jnp.cumsum(ac, axis=1)
        rel = jnp.where(causal[None, :, :, None], G[:, :, None] - G[:, None], MASK_NEG)
        M = jnp.einsum("bthn,bshn->btsh", Cc, Bc) * jnp.exp(rel) * dtc[:, None]
        y = jnp.einsum("btsh,bshp->bthp", M, xc) + jnp.einsum("bthn,bhpn->bthp", Cc, S) * jnp.exp(G)[..., None]
        GL = G[:, -1]
        S = jnp.exp(GL)[:, :, None, None] * S + jnp.einsum("bsh,bshp,bshn->bhpn", jnp.exp(GL[:, None] - G) * dtc, xc, Bc)
        return S, y

    S, y = lax.scan(step, S0.astype(jnp.float32),
                    (to_chunks(x, L), to_chunks(dt, L), to_chunks(a, L), to_chunks(Bm, L), to_chunks(Cm, L)))
    return from_chunks(y), S


def gated_rmsnorm(y, z, g):
    b, t, _ = y.shape
    u = (y * jax.nn.silu(z.astype(jnp.float32))).reshape(b, t, C_GROUPS, C_DINNER // C_GROUPS)
    u = u * lax.rsqrt(jnp.mean(u * u, axis=-1, keepdims=True) + EPS)
    return u.reshape(b, t, C_DINNER) * g.astype(jnp.float32)


def window_attend(q, k, v, mask, sinks):
    b, tq = q.shape[:2]
    qg = q.reshape(b, tq, D_KV_HEADS, D_GROUP, D_HD)
    s = jnp.einsum("bqgrd,bkgd->bgrqk", qg, k).astype(jnp.float32) * (D_HD ** -0.5)
    s = jnp.where(mask, s, MASK_NEG)
    sink = sinks.astype(jnp.float32).reshape(1, D_KV_HEADS, D_GROUP, 1, 1)
    m = jnp.maximum(jnp.max(s, axis=-1, keepdims=True), sink)
    p = jnp.exp(s - m)
    p = p / (jnp.sum(p, axis=-1, keepdims=True) + jnp.exp(sink - m))
    o = jnp.einsum("bgrqk,bkgd->bqgrd", p.astype(v.dtype), v)
    return o.reshape(b, tq, D_HEADS, D_HD)


def swa_prompt(q, k, v, sinks):
    b, t = q.shape[:2]
    nb = t // WINDOW
    kb = k.reshape(b, nb, WINDOW, D_KV_HEADS, D_HD)
    vb = v.reshape(b, nb, WINDOW, D_KV_HEADS, D_HD)
    kk = jnp.concatenate([jnp.concatenate([jnp.zeros_like(kb[:, :1]), kb[:, :-1]], axis=1), kb], axis=2)
    vv = jnp.concatenate([jnp.concatenate([jnp.zeros_like(vb[:, :1]), vb[:, :-1]], axis=1), vb], axis=2)
    q_rel = WINDOW + jnp.arange(WINDOW)
    k_rel = jnp.arange(2 * WINDOW)
    band = (k_rel[None, :] <= q_rel[:, None]) & (k_rel[None, :] > q_rel[:, None] - WINDOW)
    valid = (jnp.arange(nb)[:, None] > 0) | (k_rel[None, :] >= WINDOW)
    mask = band[None] & valid[:, None, :]
    o = window_attend(q.reshape(b * nb, WINDOW, D_HEADS, D_HD), kk.reshape(b * nb, 2 * WINDOW, D_KV_HEADS, D_HD),
                      vv.reshape(b * nb, 2 * WINDOW, D_KV_HEADS, D_HD), jnp.tile(mask, (b, 1, 1))[:, None, None], sinks)
    return o.reshape(b, t, D_HEADS, D_HD)


def swa_sample(q, k, v, kc, vc, sinks):
    t = q.shape[1]
    w0 = kc.shape[1]
    kk = jnp.concatenate([kc.astype(k.dtype), k], axis=1)
    vv = jnp.concatenate([vc.astype(v.dtype), v], axis=1)
    q_rel = w0 + jnp.arange(t)
    k_rel = jnp.arange(w0 + t)
    mask = (k_rel[None, :] <= q_rel[:, None]) & (k_rel[None, :] > q_rel[:, None] - WINDOW)
    o = window_attend(q, kk, vv, mask, sinks)
    return o, kk[:, -WINDOW:], vv[:, -WINDOW:]


def even_mixer(h, pos, layer, e, w, past):
    b, t, _ = h.shape
    aq, ak, av, bq, bf, bi, bg = split_cols(h @ w["w_in_e"][e], (A_QKW, A_QKW, A_W, B_KW, B_KW, B_W, B_W))
    q = rope(aq.reshape(b, t, A_HEADS * 2, A_QK), pos).reshape(b, t, A_HEADS, 2, A_QK)
    k = rope(ak.reshape(b, t, A_HEADS * 2, A_QK), pos).reshape(b, t, A_HEADS, 2, A_QK)
    v = av.reshape(b, t, A_HEADS, A_VD)
    lam, lam_init = diff_lambda(w["a_lambda"][e], layer)
    if past is None:
        o = diff_attn_prompt(q, k, v, lam)
        S0 = jnp.zeros((b, B_HEADS, B_DK, B_DV), jnp.float32)
    else:
        pt = past["page_table"]
        p_len = pt.shape[1] * PAGE_SIZE
        k_past = past["a_k"][e, pt].reshape(b, p_len, A_HEADS, 2, A_QK).astype(k.dtype)
        v_past = past["a_v"][e, pt].reshape(b, p_len, A_HEADS, A_VD).astype(v.dtype)
        o = diff_attn_sample(q, k, v, k_past, v_past, lam)
        S0 = past["b_state"][e]
    a_out = (rmsnorm(o, w["a_subln"][e]) * (1.0 - lam_init)).astype(h.dtype)
    lb = hgrn_lower_bound(w["b_lower_bounds"])[e].reshape(B_HEADS, B_DK)
    ob, S_new = hgrn2_scan(bq.reshape(b, t, B_HEADS, B_DK), bf.reshape(b, t, B_HEADS, B_DK),
                           bi.reshape(b, t, B_HEADS, B_DV), lb, S0)
    b_out = (rmsnorm(ob, w["b_norm"][e]) * jax.nn.sigmoid(bg.reshape(b, t, B_HEADS, B_DV).astype(jnp.float32))).astype(h.dtype)
    mix = jnp.concatenate([a_out.reshape(b, t, A_W), b_out.reshape(b, t, B_W)], axis=-1) @ w["w_out_e"][e]
    return mix, (k.reshape(b, t, A_HEADS, 2 * A_QK), v, S_new)


def odd_mixer(h, pos, e, w, past):
    b, t, _ = h.shape
    f32 = jnp.float32
    z, xbc, dt, dq, dk, dv = split_cols(h @ w["w_in_o"][e],
                                        (C_DINNER, C_CONV_DIM, C_HEADS, D_HEADS * D_HD, D_KV_HEADS * D_HD, D_KV_HEADS * D_HD))
    if past is None:
        conv0 = jnp.zeros((b, C_CONV - 1, C_CONV_DIM), xbc.dtype)
        S0 = jnp.zeros((b, C_HEADS, C_HEADDIM, C_DSTATE), f32)
    else:
        conv0 = past["c_conv"][e].astype(xbc.dtype)
        S0 = past["c_ssm"][e]
    xbc, conv_new = causal_conv(xbc, conv0, w["c_conv_w"][e], w["c_conv_b"][e])
    xs, Bm, Cm = split_cols(xbc, (C_DINNER, C_GROUPS * C_DSTATE, C_GROUPS * C_DSTATE))
    rep = C_HEADS // C_GROUPS
    xs = xs.reshape(b, t, C_HEADS, C_HEADDIM).astype(f32)
    Bm = jnp.repeat(Bm.reshape(b, t, C_GROUPS, C_DSTATE).astype(f32), rep, axis=2)
    Cm = jnp.repeat(Cm.reshape(b, t, C_GROUPS, C_DSTATE).astype(f32), rep, axis=2)
    dt = jax.nn.softplus(dt.astype(f32) + w["c_dt_bias"][e].astype(f32))
    A = -jnp.exp(w["c_a_log"][e].astype(f32))
    y, S_new = ssd_scan(xs, dt, Bm, Cm, A, S0)
    y = y + w["c_d"][e].astype(f32)[:, None] * xs
    y = gated_rmsnorm(y.reshape(b, t, C_DINNER), z, w["c_norm"][e]).astype(h.dtype)
    q = rope(dq.reshape(b, t, D_HEADS, D_HD), pos)
    k = rope(dk.reshape(b, t, D_KV_HEADS, D_HD), pos)
    v = dv.reshape(b, t, D_KV_HEADS, D_HD)
    if past is None:
        o = swa_prompt(q, k, v, w["d_sinks"][e])
        k_win, v_win = k[:, -WINDOW:], v[:, -WINDOW:]
    else:
        o, k_win, v_win = swa_sample(q, k, v, past["d_k"][e], past["d_v"][e], w["d_sinks"][e])
    mix = jnp.concatenate([y, o.reshape(b, t, D_HEADS * D_HD)], axis=-1) @ w["w_out_o"][e]
    return mix, (S_new, conv_new, k_win, v_win)


def run_trunk(x, pos, w, past):
    st = {n: [] for n in ("a_k", "a_v", "b_state", "c_ssm", "c_conv", "d_k", "d_v")}
    for l in range(DEPTH):
        g = w["norm_w"][l]
        h = rmsnorm(x, g[0])
        e = l // 2
        if l % 2 == 0:
            mix, (ak, av, bs) = even_mixer(h, pos, l, e, w, past)
            st["a_k"].append(ak)
            st["a_v"].append(av)
            st["b_state"].append(bs)
        else:
            mix, (cs, cc, dk, dv) = odd_mixer(h, pos, e, w, past)
            st["c_ssm"].append(cs)
            st["c_conv"].append(cc)
            st["d_k"].append(dk)
            st["d_v"].append(dv)
        x = x + rmsnorm(mix, g[1])
        h = rmsnorm(x, g[2])
        f = jnp.square(jax.nn.relu(h @ w["mlp_w1"][l])) @ w["mlp_w2"][l]
        x = x + rmsnorm(f, g[3])
    return x, {n: jnp.stack(v, axis=0) for n, v in st.items()}


def setup_inputs(seed: int = 0) -> dict:
    key = jax.random.key(seed)
    ks = jax.random.split(key, 32)
    f32 = jnp.float32

    def nrm(k, shape, s):
        return s * jax.random.normal(k, shape, f32)

    n_pages = PAST_LEN // PAGE_SIZE
    n_pool = (DEC_BATCH * n_pages * 5) // 4
    page_table = jax.random.permutation(ks[0], n_pool)[: DEC_BATCH * n_pages].reshape(DEC_BATCH, n_pages).astype(jnp.int32)
    dt0 = jnp.exp(jax.random.uniform(ks[1], (N_C_LAYERS, C_HEADS), f32, math.log(1e-3), math.log(1e-1)))
    return {
        "x_prompt": nrm(ks[2], (BATCH, SEQ, D_MODEL), 1.0),
        "x_sample": nrm(ks[3], (DEC_BATCH, DEC_SEQ, D_MODEL), 1.0),
        "cache_a_k": nrm(ks[4], (N_A_LAYERS, n_pool, PAGE_SIZE, A_HEADS, 2 * A_QK), 1.0),
        "cache_a_v": nrm(ks[5], (N_A_LAYERS, n_pool, PAGE_SIZE, A_HEADS, A_VD), 1.0),
        "state_b": nrm(ks[6], (N_A_LAYERS, DEC_BATCH, B_HEADS, B_DK, B_DV), 0.5),
        "state_c_ssm": nrm(ks[7], (N_C_LAYERS, DEC_BATCH, C_HEADS, C_HEADDIM, C_DSTATE), 0.5),
        "state_c_conv": nrm(ks[8], (N_C_LAYERS, DEC_BATCH, C_CONV - 1, C_CONV_DIM), 1.0),
        "cache_d_k": nrm(ks[9], (N_C_LAYERS, DEC_BATCH, WINDOW, D_KV_HEADS, D_HD), 1.0),
        "cache_d_v": nrm(ks[10], (N_C_LAYERS, DEC_BATCH, WINDOW, D_KV_HEADS, D_HD), 1.0),
        "page_table": page_table,
        "norm_w": 1.0 + nrm(ks[11], (DEPTH, 4, D_MODEL), 0.05),
        "mlp_w1": nrm(ks[12], (DEPTH, D_MODEL, D_FF), D_MODEL ** -0.5),
        "mlp_w2": nrm(ks[13], (DEPTH, D_FF, D_MODEL), D_FF ** -0.5),
        "w_in_e": nrm(ks[14], (N_A_LAYERS, D_MODEL, E_IN), D_MODEL ** -0.5),
        "w_out_e": nrm(ks[15], (N_A_LAYERS, E_MIX, D_MODEL), E_MIX ** -0.5),
        "a_lambda": nrm(ks[16], (N_A_LAYERS, 4, A_QK), 0.1),
        "a_subln": 1.0 + nrm(ks[17], (N_A_LAYERS, A_VD), 0.05),
        "b_lower_bounds": nrm(ks[18], (N_A_LAYERS, B_KW), 0.5),
        "b_norm": 1.0 + nrm(ks[19], (N_A_LAYERS, B_DV), 0.05),
        "w_in_o": nrm(ks[20], (N_C_LAYERS, D_MODEL, O_IN), D_MODEL ** -0.5),
        "w_out_o": nrm(ks[21], (N_C_LAYERS, O_MIX, D_MODEL), O_MIX ** -0.5),
        "c_conv_w": nrm(ks[22], (N_C_LAYERS, C_CONV, C_CONV_DIM), C_CONV ** -0.5),
        "c_conv_b": nrm(ks[23], (N_C_LAYERS, C_CONV_DIM), 0.02),
        "c_dt_bias": dt0 + jnp.log(-jnp.expm1(-dt0)),
        "c_a_log": jnp.log(jax.random.uniform(ks[24], (N_C_LAYERS, C_HEADS), f32, 1.0, 16.0)),
        "c_d": 1.0 + nrm(ks[25], (N_C_LAYERS, C_HEADS), 0.1),
        "c_norm": 1.0 + nrm(ks[26], (N_C_LAYERS, C_DINNER), 0.05),
        "d_sinks": nrm(ks[27], (N_C_LAYERS, D_HEADS), 0.5),
    }


def reference(x_prompt, x_sample, cache_a_k, cache_a_v, state_b, state_c_ssm, state_c_conv, cache_d_k, cache_d_v,
              page_table, norm_w, mlp_w1, mlp_w2, w_in_e, w_out_e, a_lambda, a_subln, b_lower_bounds, b_norm,
              w_in_o, w_out_o, c_conv_w, c_conv_b, c_dt_bias, c_a_log, c_d, c_norm, d_sinks):
    w = {"norm_w": norm_w, "mlp_w1": mlp_w1, "mlp_w2": mlp_w2, "w_in_e": w_in_e, "w_out_e": w_out_e,
         "a_lambda": a_lambda, "a_subln": a_subln, "b_lower_bounds": b_lower_bounds, "b_norm": b_norm,
         "w_in_o": w_in_o, "w_out_o": w_out_o, "c_conv_w": c_conv_w, "c_conv_b": c_conv_b,
         "c_dt_bias": c_dt_bias, "c_a_log": c_a_log, "c_d": c_d, "c_norm": c_norm, "d_sinks": d_sinks}
    pos_p = jnp.arange(x_prompt.shape[1])
    past_len = page_table.shape[1] * PAGE_SIZE
    pos_s = past_len + jnp.arange(x_sample.shape[1])
    y_prompt, sp = run_trunk(x_prompt, pos_p, w, None)
    past = {"page_table": page_table, "a_k": cache_a_k, "a_v": cache_a_v, "b_state": state_b,
            "c_ssm": state_c_ssm, "c_conv": state_c_conv, "d_k": cache_d_k, "d_v": cache_d_v}
    y_sample, ss = run_trunk(x_sample, pos_s, w, past)
    return (y_prompt, y_sample,
            sp["a_k"], ss["a_k"], sp["a_v"], ss["a_v"],
            sp["b_state"], ss["b_state"], sp["c_ssm"], ss["c_ssm"],
            sp["c_conv"], ss["c_conv"], sp["d_k"], ss["d_k"], sp["d_v"], ss["d_v"])
```

```python
import functools
import math

import jax
import jax.numpy as jnp
import numpy as np
from jax import lax
from jax.experimental import pallas as pl
from jax.experimental.pallas import tpu as pltpu

F32 = jnp.float32
BF16 = jnp.bfloat16

D_MODEL = 1024
DEPTH = 4
PAGE_SIZE = 128
D_FF = 4 * D_MODEL
EPS = 1e-6
ROPE_THETA = 10000.0
MASK_NEG = -1e30
LB_FLOOR = 1e-30

A_HEADS = D_MODEL // 256
A_QK = 64
A_VD = 2 * A_QK
A_Q_BLOCK = 128
A_QKW = A_HEADS * 2 * A_QK
A_W = A_HEADS * A_VD
B_HEADS = 4
B_DK = D_MODEL // 8
B_DV = D_MODEL // 8
B_CHUNK = 64
B_KW = B_HEADS * B_DK
B_W = B_HEADS * B_DV
C_HEADS = 16
C_HEADDIM = 64
C_DINNER = C_HEADS * C_HEADDIM
C_GROUPS = 2
C_DSTATE = 128
C_CONV = 4
C_CHUNK = 128
C_CONV_DIM = C_DINNER + 2 * C_GROUPS * C_DSTATE
D_HEADS = 8
D_KV_HEADS = 2
D_GROUP = D_HEADS // D_KV_HEADS
D_HD = 64
WINDOW = 128

LANES = 128
ROPE_HALF = 32
DT_PAD = LANES
VMEM_LIMIT = 56 * 1024 * 1024
ROW_TILE = 512


def _rms(x, g):
    return x * lax.rsqrt(jnp.mean(x * x, axis=-1, keepdims=True) + EPS) * g


def _rope_tile(y, cos, sin_lo, sin_hi):
    up = pltpu.roll(y, LANES - ROPE_HALF, 1)
    down = pltpu.roll(y, ROPE_HALF, 1)
    return y * cos + up * sin_lo + down * sin_hi


def _proj_kernel(x_ref, g_ref, w_ref, cos_ref, slo_ref, shi_ref, *out_refs, widths, rope_flags):
    h = _rms(x_ref[...], g_ref[...]).astype(BF16)
    cos = cos_ref[...]
    slo = slo_ref[...]
    shi = shi_ref[...]
    off = 0
    for o_ref, width, roped in zip(out_refs, widths, rope_flags):
        y = jnp.dot(h, w_ref[:, off:off + width], preferred_element_type=F32)
        if roped:
            for c in range(0, width, LANES):
                o_ref[:, c:c + LANES] = _rope_tile(y[:, c:c + LANES], cos, slo, shi)
        else:
            o_ref[...] = y
        off += width


def _in_projection(x, g, w, rope_tabs, widths, rope_flags):
    m = x.shape[0]
    n = w.shape[1]
    assert sum(widths) == n and m % ROW_TILE == 0
    row = lambda i: (i, 0)
    const = lambda i: (0, 0)
    kern = functools.partial(_proj_kernel, widths=tuple(widths), rope_flags=tuple(rope_flags))
    return pl.pallas_call(
        kern,
        grid=(m // ROW_TILE,),
        in_specs=[
            pl.BlockSpec((ROW_TILE, D_MODEL), row),
            pl.BlockSpec((1, D_MODEL), const),
            pl.BlockSpec((D_MODEL, n), const),
            pl.BlockSpec((ROW_TILE, LANES), row),
            pl.BlockSpec((ROW_TILE, LANES), row),
            pl.BlockSpec((ROW_TILE, LANES), row),
        ],
        out_specs=[pl.BlockSpec((ROW_TILE, wd), row) for wd in widths],
        out_shape=[jax.ShapeDtypeStruct((m, wd), F32) for wd in widths],
        compiler_params=pltpu.CompilerParams(dimension_semantics=("arbitrary",), vmem_limit_bytes=VMEM_LIMIT),
        name="in_projection",
    )(x, g, w, *rope_tabs)


FF_TILE = 1024


def _post_kernel(mix_ref, x_ref, wo_ref, g_ref, w1_ref, w2_ref, o_ref, x1_s, h_s, acc_s):
    j = pl.program_id(1)

    @pl.when(j == 0)
    def _():
        m = jnp.dot(mix_ref[...].astype(BF16), wo_ref[...], preferred_element_type=F32)
        x1 = x_ref[...] + _rms(m, g_ref[1:2, :])
        x1_s[...] = x1
        h_s[...] = _rms(x1, g_ref[2:3, :]).astype(BF16)
        acc_s[...] = jnp.zeros_like(acc_s)

    a = jnp.dot(h_s[...], w1_ref[...], preferred_element_type=F32)
    a = jnp.square(jnp.maximum(a, 0.0)).astype(BF16)
    acc_s[...] += jnp.dot(a, w2_ref[...], preferred_element_type=F32)

    @pl.when(j == pl.num_programs(1) - 1)
    def _():
        o_ref[...] = x1_s[...] + _rms(acc_s[...], g_ref[3:4, :])


def _post_mixer(mix, x, w_out, g, w1, w2):
    m = x.shape[0]
    kmix = mix.shape[1]
    return pl.pallas_call(
        _post_kernel,
        grid=(m // ROW_TILE, D_FF // FF_TILE),
        in_specs=[
            pl.BlockSpec((ROW_TILE, kmix), lambda i, j: (i, 0)),
            pl.BlockSpec((ROW_TILE, D_MODEL), lambda i, j: (i, 0)),
            pl.BlockSpec((kmix, D_MODEL), lambda i, j: (0, 0)),
            pl.BlockSpec((4, D_MODEL), lambda i, j: (0, 0)),
            pl.BlockSpec((D_MODEL, FF_TILE), lambda i, j: (0, j)),
            pl.BlockSpec((FF_TILE, D_MODEL), lambda i, j: (j, 0)),
        ],
        out_specs=pl.BlockSpec((ROW_TILE, D_MODEL), lambda i, j: (i, 0)),
        out_shape=jax.ShapeDtypeStruct((m, D_MODEL), F32),
        scratch_shapes=[
            pltpu.VMEM((ROW_TILE, D_MODEL), F32),
            pltpu.VMEM((ROW_TILE, D_MODEL), BF16),
            pltpu.VMEM((ROW_TILE, D_MODEL), F32),
        ],
        compiler_params=pltpu.CompilerParams(dimension_semantics=("arbitrary", "arbitrary"),
                                             vmem_limit_bytes=VMEM_LIMIT),
        name="post_mixer_mlp",
    )(mix, x, w_out, g, w1, w2)


def _rmsnorm_j(x, g):
    xf = x.astype(F32)
    r = xf * lax.rsqrt(jnp.mean(xf * xf, axis=-1, keepdims=True) + EPS)
    return (r * g.astype(F32)).astype(x.dtype)


def _to_chunks(a, L):
    b, t = a.shape[:2]
    return jnp.moveaxis(a.reshape((b, t // L, L) + a.shape[2:]), 1, 0)


def _from_chunks(a):
    n, b, L = a.shape[:3]
    return jnp.moveaxis(a, 0, 1).reshape((b, n * L) + a.shape[3:])


def _diff_lambda(lam, layer):
    lam = lam.astype(F32)
    lam_init = 0.8 - 0.6 * math.exp(-0.3 * layer)
    lam_full = jnp.exp(jnp.sum(lam[0] * lam[1])) - jnp.exp(jnp.sum(lam[2] * lam[3])) + lam_init
    return lam_full, lam_init


def _diff_attend(q, parts, lam):
    scale = A_QK ** -0.5
    scores = []
    for k, _, m in parts:
        s = jnp.einsum("bqhcd,bkhcd->bhcqk", q, k).astype(F32) * scale
        scores.append(s if m is None else jnp.where(m, s, MASK_NEG))
    p = jax.nn.softmax(jnp.concatenate(scores, axis=-1), axis=-1)
    p = p[:, :, 0] - lam * p[:, :, 1]
    out = None
    off = 0
    for k, v, _ in parts:
        n = k.shape[1]
        o = jnp.einsum("bhqk,bkhd->bqhd", p[..., off:off + n].astype(v.dtype), v)
        out = o if out is None else out + o
        off += n
    return out


def _diff_attn_prompt(q, k, v, lam):
    b, t = q.shape[:2]
    nb = t // A_Q_BLOCK
    qb = jnp.moveaxis(q.reshape(b, nb, A_Q_BLOCK, A_HEADS, 2, A_QK), 1, 0)
    k_pos = jnp.arange(t)

    def block(args):
        qi, i = args
        q_pos = i * A_Q_BLOCK + jnp.arange(A_Q_BLOCK)
        mask = k_pos[None, :] <= q_pos[:, None]
        return _diff_attend(qi, [(k, v, mask)], lam)

    o = lax.map(block, (qb, jnp.arange(nb)))
    return jnp.moveaxis(o, 0, 1).reshape(b, t, A_HEADS, A_VD)


def _diff_attn_sample(q, k, v, k_past, v_past, lam):
    t = q.shape[1]
    causal = jnp.tril(jnp.ones((t, t), bool))
    return _diff_attend(q, [(k_past, v_past, None), (k, v, causal)], lam)


def _hgrn_lower_bound(p):
    sm = jax.nn.softmax(p.astype(F32), axis=0)
    return jnp.concatenate([jnp.zeros_like(sm[:1]), jnp.cumsum(sm[1:], axis=0)], axis=0)


def _hgrn2_scan(q, f_logit, i, lb, S0):
    t = q.shape[1]
    L = math.gcd(t, B_CHUNK)
    z = f_logit.astype(F32)
    log_lb = jnp.log(jnp.maximum(lb, LB_FLOOR))
    log_f = jnp.logaddexp(log_lb, jnp.log1p(-lb) + jax.nn.log_sigmoid(z))
    k = (1.0 - lb) * jax.nn.sigmoid(-z)
    qf = jax.nn.silu(q.astype(F32))
    vf = i.astype(F32)
    causal = jnp.tril(jnp.ones((L, L), bool))

    def step(S, c):
        qc, kc, vc, gc = c
        G = jnp.cumsum(gc, axis=1)
        o_inter = jnp.einsum("blhk,bhkv->blhv", qc * jnp.exp(G), S)
        rel = jnp.where(causal[None, :, :, None, None], G[:, :, None] - G[:, None], MASK_NEG)
        A = jnp.einsum("bthk,bshk,btshk->bhts", qc, kc, jnp.exp(rel))
        o = o_inter + jnp.einsum("bhts,bshv->bthv", A, vc)
        GL = G[:, -1]
        S = jnp.exp(GL)[..., None] * S + jnp.einsum("bshk,bshv->bhkv", kc * jnp.exp(GL[:, None] - G), vc)
        return S, o

    S, o = lax.scan(step, S0.astype(F32), (_to_chunks(qf, L), _to_chunks(k, L), _to_chunks(vf, L), _to_chunks(log_f, L)))
    return _from_chunks(o), S


def _causal_conv(xbc, conv0, w, b):
    xp = jnp.concatenate([conv0, xbc], axis=1)
    y = lax.conv_general_dilated(xp, w.astype(xp.dtype)[:, None, :], window_strides=(1,), padding="VALID",
                                 dimension_numbers=("NWC", "WIO", "NWC"), feature_group_count=xp.shape[-1])
    return jax.nn.silu(y + b.astype(xp.dtype)), xp[:, -(C_CONV - 1):]


def _ssd_scan(x, dt, Bm, Cm, A, S0):
    t = x.shape[1]
    L = math.gcd(t, C_CHUNK)
    a = dt * A
    causal = jnp.tril(jnp.ones((L, L), bool))

    def step(S, c):
        xc, dtc, ac, Bc, Cc = c
        G = jnp.cumsum(ac, axis=1)
        rel = jnp.where(causal[None, :, :, None], G[:, :, None] - G[:, None], MASK_NEG)
        M = jnp.einsum("bthn,bshn->btsh", Cc, Bc) * jnp.exp(rel) * dtc[:, None]
        y = jnp.einsum("btsh,bshp->bthp", M, xc) + jnp.einsum("bthn,bhpn->bthp", Cc, S) * jnp.exp(G)[..., None]
        GL = G[:, -1]
        S = jnp.exp(GL)[:, :, None, None] * S + jnp.einsum("bsh,bshp,bshn->bhpn", jnp.exp(GL[:, None] - G) * dtc, xc, Bc)
        return S, y

    S, y = lax.scan(step, S0.astype(F32),
                    (_to_chunks(x, L), _to_chunks(dt, L), _to_chunks(a, L), _to_chunks(Bm, L), _to_chunks(Cm, L)))
    return _from_chunks(y), S


def _gated_rmsnorm(y, z, g):
    b, t, _ = y.shape
    u = (y * jax.nn.silu(z.astype(F32))).reshape(b, t, C_GROUPS, C_DINNER // C_GROUPS)
    u = u * lax.rsqrt(jnp.mean(u * u, axis=-1, keepdims=True) + EPS)
    return u.reshape(b, t, C_DINNER) * g.astype(F32)


def _window_attend(q, k, v, mask, sinks):
    b, tq = q.shape[:2]
    qg = q.reshape(b, tq, D_KV_HEADS, D_GROUP, D_HD)
    s = jnp.einsum("bqgrd,bkgd->bgrqk", qg, k).astype(F32) * (D_HD ** -0.5)
    s = jnp.where(mask, s, MASK_NEG)
    sink = sinks.astype(F32).reshape(1, D_KV_HEADS, D_GROUP, 1, 1)
    m = jnp.maximum(jnp.max(s, axis=-1, keepdims=True), sink)
    p = jnp.exp(s - m)
    p = p / (jnp.sum(p, axis=-1, keepdims=True) + jnp.exp(sink - m))
    o = jnp.einsum("bgrqk,bkgd->bqgrd", p.astype(v.dtype), v)
    return o.reshape(b, tq, D_HEADS, D_HD)


def _swa_prompt(q, k, v, sinks):
    b, t = q.shape[:2]
    nb = t // WINDOW
    kb = k.reshape(b, nb, WINDOW, D_KV_HEADS, D_HD)
    vb = v.reshape(b, nb, WINDOW, D_KV_HEADS, D_HD)
    kk = jnp.concatenate([jnp.concatenate([jnp.zeros_like(kb[:, :1]), kb[:, :-1]], axis=1), kb], axis=2)
    vv = jnp.concatenate([jnp.concatenate([jnp.zeros_like(vb[:, :1]), vb[:, :-1]], axis=1), vb], axis=2)
    q_rel = WINDOW + jnp.arange(WINDOW)
    k_rel = jnp.arange(2 * WINDOW)
    band = (k_rel[None, :] <= q_rel[:, None]) & (k_rel[None, :] > q_rel[:, None] - WINDOW)
    valid = (jnp.arange(nb)[:, None] > 0) | (k_rel[None, :] >= WINDOW)
    mask = band[None] & valid[:, None, :]
    o = _window_attend(q.reshape(b * nb, WINDOW, D_HEADS, D_HD), kk.reshape(b * nb, 2 * WINDOW, D_KV_HEADS, D_HD),
                       vv.reshape(b * nb, 2 * WINDOW, D_KV_HEADS, D_HD), jnp.tile(mask, (b, 1, 1))[:, None, None], sinks)
    return o.reshape(b, t, D_HEADS, D_HD)


def _swa_sample(q, k, v, kc, vc, sinks):
    t = q.shape[1]
    w0 = kc.shape[1]
    kk = jnp.concatenate([kc.astype(k.dtype), k], axis=1)
    vv = jnp.concatenate([vc.astype(v.dtype), v], axis=1)
    q_rel = w0 + jnp.arange(t)
    k_rel = jnp.arange(w0 + t)
    mask = (k_rel[None, :] <= q_rel[:, None]) & (k_rel[None, :] > q_rel[:, None] - WINDOW)
    o = _window_attend(q, kk, vv, mask, sinks)
    return o, kk[:, -WINDOW:], vv[:, -WINDOW:]


def _even_mixer(parts, b, t, layer, e, w, past):
    aq, ak, av, bq, bf, bi, bg = parts
    q = aq.reshape(b, t, A_HEADS, 2, A_QK)
    k = ak.reshape(b, t, A_HEADS, 2, A_QK)
    v = av.reshape(b, t, A_HEADS, A_VD)
    lam, lam_init = _diff_lambda(w["a_lambda"][e], layer)
    if past is None:
        o = _diff_attn_prompt(q, k, v, lam)
        S0 = jnp.zeros((b, B_HEADS, B_DK, B_DV), F32)
    else:
        pt = past["page_table"]
        p_len = pt.shape[1] * PAGE_SIZE
        k_past = past["a_k"][e, pt].reshape(b, p_len, A_HEADS, 2, A_QK)
        v_past = past["a_v"][e, pt].reshape(b, p_len, A_HEADS, A_VD)
        o = _diff_attn_sample(q, k, v, k_past, v_past, lam)
        S0 = past["b_state"][e]
    a_out = _rmsnorm_j(o, w["a_subln"][e]) * (1.0 - lam_init)
    lb = _hgrn_lower_bound(w["b_lower_bounds"])[e].reshape(B_HEADS, B_DK)
    ob, S_new = _hgrn2_scan(bq.reshape(b, t, B_HEADS, B_DK), bf.reshape(b, t, B_HEADS, B_DK),
                            bi.reshape(b, t, B_HEADS, B_DV), lb, S0)
    b_out = _rmsnorm_j(ob, w["b_norm"][e]) * jax.nn.sigmoid(bg.reshape(b, t, B_HEADS, B_DV))
    mix = jnp.concatenate([a_out.reshape(b * t, A_W), b_out.reshape(b * t, B_W)], axis=-1)
    return mix, (k.reshape(b, t, A_HEADS, 2 * A_QK), v, S_new)


def _odd_mixer(parts, b, t, e, w, past):
    z, xbc, dt, dq, dk, dv = parts
    z = z.reshape(b, t, C_DINNER)
    xbc = xbc.reshape(b, t, C_CONV_DIM)
    dt = dt[:, :C_HEADS].reshape(b, t, C_HEADS)
    if past is None:
        conv0 = jnp.zeros((b, C_CONV - 1, C_CONV_DIM), xbc.dtype)
        S0 = jnp.zeros((b, C_HEADS, C_HEADDIM, C_DSTATE), F32)
    else:
        conv0 = past["c_conv"][e]
        S0 = past["c_ssm"][e]
    xbc, conv_new = _causal_conv(xbc, conv0, w["c_conv_w"][e], w["c_conv_b"][e])
    xs, Bm, Cm = jnp.split(xbc, [C_DINNER, C_DINNER + C_GROUPS * C_DSTATE], axis=-1)
    rep = C_HEADS // C_GROUPS
    xs = xs.reshape(b, t, C_HEADS, C_HEADDIM)
    Bm = jnp.repeat(Bm.reshape(b, t, C_GROUPS, C_DSTATE), rep, axis=2)
    Cm = jnp.repeat(Cm.reshape(b, t, C_GROUPS, C_DSTATE), rep, axis=2)
    dt = jax.nn.softplus(dt + w["c_dt_bias"][e])
    A = -jnp.exp(w["c_a_log"][e])
    y, S_new = _ssd_scan(xs, dt, Bm, Cm, A, S0)
    y = y + w["c_d"][e][:, None] * xs
    y = _gated_rmsnorm(y.reshape(b, t, C_DINNER), z, w["c_norm"][e])
    q = dq.reshape(b, t, D_HEADS, D_HD)
    k = dk.reshape(b, t, D_KV_HEADS, D_HD)
    v = dv.reshape(b, t, D_KV_HEADS, D_HD)
    if past is None:
        o = _swa_prompt(q, k, v, w["d_sinks"][e])
        k_win, v_win = k[:, -WINDOW:], v[:, -WINDOW:]
    else:
        o, k_win, v_win = _swa_sample(q, k, v, past["d_k"][e], past["d_v"][e], w["d_sinks"][e])
    mix = jnp.concatenate([y.reshape(b * t, C_DINNER), o.reshape(b * t, D_HEADS * D_HD)], axis=-1)
    return mix, (S_new, conv_new, k_win, v_win)


def _rope_tables(pos):
    inv = ROPE_THETA ** (-jnp.arange(0, A_QK, 2, dtype=F32) / A_QK)
    ang = pos.astype(F32)[:, None] * inv[None, :]
    cos = jnp.tile(jnp.cos(ang), (1, LANES // ROPE_HALF))
    sin = jnp.tile(jnp.sin(ang), (1, LANES // ROPE_HALF))
    first_half = (jnp.arange(LANES) % (2 * ROPE_HALF)) < ROPE_HALF
    sin_lo = jnp.where(first_half[None, :], -sin, 0.0)
    sin_hi = jnp.where(first_half[None, :], 0.0, sin)
    return cos, sin_lo, sin_hi


def kernel(x_prompt, x_sample, cache_a_k, cache_a_v, state_b, state_c_ssm, state_c_conv, cache_d_k, cache_d_v,
           page_table, norm_w, mlp_w1, mlp_w2, w_in_e, w_out_e, a_lambda, a_subln, b_lower_bounds, b_norm,
           w_in_o, w_out_o, c_conv_w, c_conv_b, c_dt_bias, c_a_log, c_d, c_norm, d_sinks):
    w = {"a_lambda": a_lambda, "a_subln": a_subln, "b_lower_bounds": b_lower_bounds, "b_norm": b_norm,
         "c_conv_w": c_conv_w, "c_conv_b": c_conv_b, "c_dt_bias": c_dt_bias, "c_a_log": c_a_log,
         "c_d": c_d, "c_norm": c_norm, "d_sinks": d_sinks}
    bp, tp, _ = x_prompt.shape
    bs, ts, _ = x_sample.shape
    mp, ms = bp * tp, bs * ts
    past_len = page_table.shape[1] * PAGE_SIZE
    pos = jnp.concatenate([jnp.tile(jnp.arange(tp), bp), jnp.tile(past_len + jnp.arange(ts), bs)])
    rope_tabs = _rope_tables(pos)
    past = {"page_table": page_table, "a_k": cache_a_k, "a_v": cache_a_v, "b_state": state_b,
            "c_ssm": state_c_ssm, "c_conv": state_c_conv, "d_k": cache_d_k, "d_v": cache_d_v}

    w_in_e_b = w_in_e.astype(BF16)
    dt_off = C_DINNER + C_CONV_DIM
    w_in_o_b = jnp.concatenate(
        [w_in_o[:, :, :dt_off],
         jnp.pad(w_in_o[:, :, dt_off:dt_off + C_HEADS], ((0, 0), (0, 0), (0, DT_PAD - C_HEADS))),
         w_in_o[:, :, dt_off + C_HEADS:]], axis=-1).astype(BF16)
    w_out_e_b = w_out_e.astype(BF16)
    w_out_o_b = w_out_o.astype(BF16)
    w1_b = mlp_w1.astype(BF16)
    w2_b = mlp_w2.astype(BF16)

    even_widths = (A_QKW, A_QKW, A_W, B_KW, B_KW, B_W, B_W)
    even_rope = (True, True, False, False, False, False, False)
    odd_widths = (C_DINNER, C_CONV_DIM, DT_PAD, D_HEADS * D_HD, D_KV_HEADS * D_HD, D_KV_HEADS * D_HD)
    odd_rope = (False, False, False, True, True, False)

    x = jnp.concatenate([x_prompt.reshape(mp, D_MODEL), x_sample.reshape(ms, D_MODEL)], axis=0)
    st_p = {n: [] for n in ("a_k", "a_v", "b_state", "c_ssm", "c_conv", "d_k", "d_v")}
    st_s = {n: [] for n in st_p}
    for l in range(DEPTH):
        g = norm_w[l]
        e = l // 2
        if l % 2 == 0:
            parts = _in_projection(x, g[0:1], w_in_e_b[e], rope_tabs, even_widths, even_rope)
            mix_p, (ak, av, bst) = _even_mixer([p[:mp] for p in parts], bp, tp, l, e, w, None)
            mix_s, (aks, avs, bsts) = _even_mixer([p[mp:] for p in parts], bs, ts, l, e, w, past)
            for d, vals in ((st_p, (ak, av, bst)), (st_s, (aks, avs, bsts))):
                d["a_k"].append(vals[0])
                d["a_v"].append(vals[1])
                d["b_state"].append(vals[2])
            w_out = w_out_e_b[e]
        else:
            parts = _in_projection(x, g[0:1], w_in_o_b[e], rope_tabs, odd_widths, odd_rope)
            mix_p, vals_p = _odd_mixer([p[:mp] for p in parts], bp, tp, e, w, None)
            mix_s, vals_s = _odd_mixer([p[mp:] for p in parts], bs, ts, e, w, past)
            for d, vals in ((st_p, vals_p), (st_s, vals_s)):
                d["c_ssm"].append(vals[0])
                d["c_conv"].append(vals[1])
                d["d_k"].append(vals[2])
                d["d_v"].append(vals[3])
            w_out = w_out_o_b[e]
        mix = jnp.concatenate([mix_p, mix_s], axis=0)
        x = _post_mixer(mix, x, w_out, g, w1_b[l], w2_b[l])
    sp = {n: jnp.stack(v, axis=0) for n, v in st_p.items()}
    ss = {n: jnp.stack(v, axis=0) for n, v in st_s.items()}
    y_prompt = x[:mp].reshape(bp, tp, D_MODEL)
    y_sample = x[mp:].reshape(bs, ts, D_MODEL)
    return (y_prompt, y_sample,
            sp["a_k"], ss["a_k"], sp["a_v"], ss["a_v"],
            sp["b_state"], ss["b_state"], sp["c_ssm"], ss["c_ssm"],
            sp["c_conv"], ss["c_conv"], sp["d_k"], ss["d_k"], sp["d_v"], ss["d_v"])
```

```python
import functools
import math

import jax
import jax.numpy as jnp
import numpy as np
from jax import lax
from jax.experimental import pallas as pl
from jax.experimental.pallas import tpu as pltpu

F32 = jnp.float32
BF16 = jnp.bfloat16

D_MODEL = 1024
DEPTH = 4
PAGE_SIZE = 128
D_FF = 4 * D_MODEL
EPS = 1e-6
ROPE_THETA = 10000.0
MASK_NEG = -1e30
LB_FLOOR = 1e-30

A_HEADS = D_MODEL // 256
A_QK = 64
A_VD = 2 * A_QK
A_Q_BLOCK = 128
A_QKW = A_HEADS * 2 * A_QK
A_W = A_HEADS * A_VD
B_HEADS = 4
B_DK = D_MODEL // 8
B_DV = D_MODEL // 8
B_CHUNK = 64
B_KW = B_HEADS * B_DK
B_W = B_HEADS * B_DV
C_HEADS = 16
C_HEADDIM = 64
C_DINNER = C_HEADS * C_HEADDIM
C_GROUPS = 2
C_DSTATE = 128
C_CONV = 4
C_CHUNK = 128
C_CONV_DIM = C_DINNER + 2 * C_GROUPS * C_DSTATE
D_HEADS = 8
D_KV_HEADS = 2
D_GROUP = D_HEADS // D_KV_HEADS
D_HD = 64
WINDOW = 128

LANES = 128
ROPE_HALF = 32
DT_PAD = LANES
VMEM_LIMIT = 56 * 1024 * 1024
ROW_TILE = 512


def _rms(x, g):
    return x * lax.rsqrt(jnp.mean(x * x, axis=-1, keepdims=True) + EPS) * g


def _rope_tile(y, cos, sin_lo, sin_hi):
    up = pltpu.roll(y, LANES - ROPE_HALF, 1)
    down = pltpu.roll(y, ROPE_HALF, 1)
    return y * cos + up * sin_lo + down * sin_hi


def _proj_kernel(x_ref, g_ref, w_ref, cos_ref, slo_ref, shi_ref, *out_refs, widths, rope_flags):
    h = _rms(x_ref[...], g_ref[...]).astype(BF16)
    cos = cos_ref[...]
    slo = slo_ref[...]
    shi = shi_ref[...]
    off = 0
    for o_ref, width, roped in zip(out_refs, widths, rope_flags):
        y = jnp.dot(h, w_ref[:, off:off + width], preferred_element_type=F32)
        if roped:
            for c in range(0, width, LANES):
                o_ref[:, c:c + LANES] = _rope_tile(y[:, c:c + LANES], cos, slo, shi)
        else:
            o_ref[...] = y
        off += width


def _in_projection(x, g, w, rope_tabs, widths, rope_flags):
    m = x.shape[0]
    n = w.shape[1]
    assert sum(widths) == n and m % ROW_TILE == 0
    row = lambda i: (i, 0)
    const = lambda i: (0, 0)
    kern = functools.partial(_proj_kernel, widths=tuple(widths), rope_flags=tuple(rope_flags))
    return pl.pallas_call(
        kern,
        grid=(m // ROW_TILE,),
        in_specs=[
            pl.BlockSpec((ROW_TILE, D_MODEL), row),
            pl.BlockSpec((1, D_MODEL), const),
            pl.BlockSpec((D_MODEL, n), const),
            pl.BlockSpec((ROW_TILE, LANES), row),
            pl.BlockSpec((ROW_TILE, LANES), row),
            pl.BlockSpec((ROW_TILE, LANES), row),
        ],
        out_specs=[pl.BlockSpec((ROW_TILE, wd), row) for wd in widths],
        out_shape=[jax.ShapeDtypeStruct((m, wd), F32) for wd in widths],
        compiler_params=pltpu.CompilerParams(dimension_semantics=("arbitrary",), vmem_limit_bytes=VMEM_LIMIT),
        name="in_projection",
    )(x, g, w, *rope_tabs)


FF_TILE = 1024


def _post_kernel(mix_ref, x_ref, wo_ref, g_ref, w1_ref, w2_ref, o_ref, x1_s, h_s, acc_s):
    j = pl.program_id(1)

    @pl.when(j == 0)
    def _():
        m = jnp.dot(mix_ref[...].astype(BF16), wo_ref[...], preferred_element_type=F32)
        x1 = x_ref[...] + _rms(m, g_ref[1:2, :])
        x1_s[...] = x1
        h_s[...] = _rms(x1, g_ref[2:3, :]).astype(BF16)
        acc_s[...] = jnp.zeros_like(acc_s)

    a = jnp.dot(h_s[...], w1_ref[...], preferred_element_type=F32)
    a = jnp.square(jnp.maximum(a, 0.0)).astype(BF16)
    acc_s[...] += jnp.dot(a, w2_ref[...], preferred_element_type=F32)

    @pl.when(j == pl.num_programs(1) - 1)
    def _():
        o_ref[...] = x1_s[...] + _rms(acc_s[...], g_ref[3:4, :])


def _post_mixer(mix, x, w_out, g, w1, w2):
    m = x.shape[0]
    kmix = mix.shape[1]
    return pl.pallas_call(
        _post_kernel,
        grid=(m // ROW_TILE, D_FF // FF_TILE),
        in_specs=[
            pl.BlockSpec((ROW_TILE, kmix), lambda i, j: (i, 0)),
            pl.BlockSpec((ROW_TILE, D_MODEL), lambda i, j: (i, 0)),
            pl.BlockSpec((kmix, D_MODEL), lambda i, j: (0, 0)),
            pl.BlockSpec((4, D_MODEL), lambda i, j: (0, 0)),
            pl.BlockSpec((D_MODEL, FF_TILE), lambda i, j: (0, j)),
            pl.BlockSpec((FF_TILE, D_MODEL), lambda i, j: (j, 0)),
        ],
        out_specs=pl.BlockSpec((ROW_TILE, D_MODEL), lambda i, j: (i, 0)),
        out_shape=jax.ShapeDtypeStruct((m, D_MODEL), F32),
        scratch_shapes=[
            pltpu.VMEM((ROW_TILE, D_MODEL), F32),
            pltpu.VMEM((ROW_TILE, D_MODEL), BF16),
            pltpu.VMEM((ROW_TILE, D_MODEL), F32),
        ],
        compiler_params=pltpu.CompilerParams(dimension_semantics=("arbitrary", "arbitrary"),
                                             vmem_limit_bytes=VMEM_LIMIT),
        name="post_mixer_mlp",
    )(mix, x, w_out, g, w1, w2)


DEC_PAGES = 8
A_ROWS_PER_HEAD = 2
NT_DIMS = (((1,), (1,)), ((), ()))
TN_DIMS = (((0,), (0,)), ((), ()))


def _head_diag(pv, rows_per_head):
    return jnp.concatenate(
        [pv[h * rows_per_head:(h + 1) * rows_per_head, h * LANES:(h + 1) * LANES] for h in range(A_HEADS)], axis=0)


def _decode_kernel(pt_ref, q_ref, kn_ref, vn_ref, sub_ref, lam_ref, *rest, n_pages, t_new, out_scale):
    del pt_ref
    k_refs = rest[:n_pages]
    v_refs = rest[n_pages:2 * n_pages]
    o_ref = rest[2 * n_pages]
    qhat_s, m_s, l_s, acc_s = rest[2 * n_pages + 1:]
    j = pl.program_id(1)
    rph = A_ROWS_PER_HEAD * t_new
    rows = A_HEADS * rph

    @pl.when(j == 0)
    def _():
        qt = q_ref[...] * (A_QK ** -0.5)
        qrep = jnp.concatenate([qt] * (A_HEADS * A_ROWS_PER_HEAD), axis=0)
        row = lax.broadcasted_iota(jnp.int32, (rows, A_QKW), 0)
        lane = lax.broadcasted_iota(jnp.int32, (rows, A_QKW), 1)
        qhat = jnp.where(lane // A_QK == row // t_new, qrep, 0.0).astype(BF16)
        qhat_s[...] = qhat
        pad = jnp.zeros((PAGE_SIZE - t_new, A_QKW), F32)
        kn = jnp.concatenate([kn_ref[...], pad], axis=0).astype(BF16)
        vn = jnp.concatenate([vn_ref[...], pad], axis=0).astype(BF16)
        s = lax.dot_general(qhat, kn, NT_DIMS, preferred_element_type=F32)
        r = lax.broadcasted_iota(jnp.int32, (rows, PAGE_SIZE), 0)
        c = lax.broadcasted_iota(jnp.int32, (rows, PAGE_SIZE), 1)
        s = jnp.where(c <= r % t_new, s, MASK_NEG)
        m = jnp.max(s, axis=-1, keepdims=True)
        p = jnp.exp(s - m)
        m_s[...] = m
        l_s[...] = jnp.sum(p, axis=-1, keepdims=True)
        acc_s[...] = _head_diag(jnp.dot(p.astype(BF16), vn, preferred_element_type=F32), rph)

    qhat = qhat_s[...]
    ss = [lax.dot_general(qhat, k_refs[n][...].astype(BF16), NT_DIMS, preferred_element_type=F32)
          for n in range(n_pages)]
    smax = functools.reduce(jnp.maximum, ss)
    m_old = m_s[...]
    m_new = jnp.maximum(m_old, jnp.max(smax, axis=-1, keepdims=True))
    alpha = jnp.exp(m_old - m_new)
    ps = [jnp.exp(s - m_new) for s in ss]
    l_s[...] = alpha * l_s[...] + jnp.sum(functools.reduce(jnp.add, ps), axis=-1, keepdims=True)
    pv = None
    for n in range(n_pages):
        d = jnp.dot(ps[n].astype(BF16), v_refs[n][...].astype(BF16), preferred_element_type=F32)
        pv = d if pv is None else pv + d
    acc_s[...] = alpha * acc_s[...] + _head_diag(pv, rph)
    m_s[...] = m_new

    @pl.when(j == pl.num_programs(1) - 1)
    def _():
        out = acc_s[...] / l_s[...]
        lam = lam_ref[...]
        for h in range(A_HEADS):
            o_h = out[h * rph:h * rph + t_new] - lam * out[h * rph + t_new:(h + 1) * rph]
            o_ref[:, h * LANES:(h + 1) * LANES] = _rms(o_h, sub_ref[...]) * out_scale


def _paged_diff_attention(q, k_new, v_new, cache_k, cache_v, page_table, layer_e, lam, subln, out_scale):
    b, t_new, _ = q.shape
    n_table = page_table.shape[1]
    assert n_table % DEC_PAGES == 0
    seq = lambda i, j, pt: (i, 0, 0)
    const = lambda i, j, pt: (0, 0)

    def page_spec(n):
        return pl.BlockSpec((None, None, PAGE_SIZE, A_QKW),
                            lambda i, j, pt: (layer_e, pt[i, j * DEC_PAGES + n], 0, 0))

    rows = A_HEADS * A_ROWS_PER_HEAD * t_new
    kern = functools.partial(_decode_kernel, n_pages=DEC_PAGES, t_new=t_new, out_scale=out_scale)
    grid_spec = pltpu.PrefetchScalarGridSpec(
        num_scalar_prefetch=1,
        grid=(b, n_table // DEC_PAGES),
        in_specs=[pl.BlockSpec((None, t_new, A_QKW), seq),
                  pl.BlockSpec((None, t_new, A_QKW), seq),
                  pl.BlockSpec((None, t_new, A_W), seq),
                  pl.BlockSpec((1, A_VD), const),
                  pl.BlockSpec((1, A_VD), const)]
                 + [page_spec(n) for n in range(DEC_PAGES)] * 2,
        out_specs=pl.BlockSpec((None, t_new, A_W), seq),
        scratch_shapes=[pltpu.VMEM((rows, A_QKW), BF16),
                        pltpu.VMEM((rows, 1), F32),
                        pltpu.VMEM((rows, 1), F32),
                        pltpu.VMEM((rows, A_VD), F32)],
    )
    lam_row = jnp.full((1, A_VD), lam, F32)
    return pl.pallas_call(
        kern,
        grid_spec=grid_spec,
        out_shape=jax.ShapeDtypeStruct((b, t_new, A_W), F32),
        compiler_params=pltpu.CompilerParams(dimension_semantics=("arbitrary", "arbitrary"),
                                             vmem_limit_bytes=VMEM_LIMIT),
        name="paged_diff_attention",
    )(page_table, q, k_new, v_new, subln.reshape(1, A_VD), lam_row,
      *([cache_k] * DEC_PAGES), *([cache_v] * DEC_PAGES))


ATT_TILE = 512


def _flash_kernel(q_ref, k_ref, v_ref, sub_ref, lam_ref, o_ref, qhat_s, m_s, l_s, acc_s, *, out_scale):
    qi = pl.program_id(2)
    ki = pl.program_id(3)
    tq = q_ref.shape[0]
    tk = k_ref.shape[0]

    @pl.when(ki == 0)
    def _():
        qt = q_ref[...] * (A_QK ** -0.5)
        lane = lax.broadcasted_iota(jnp.int32, (tq, LANES), 1)
        qhat_s[0:tq, :] = jnp.where(lane < A_QK, qt, 0.0).astype(BF16)
        qhat_s[tq:2 * tq, :] = jnp.where(lane < A_QK, 0.0, qt).astype(BF16)
        m_s[...] = jnp.full(m_s.shape, MASK_NEG, F32)
        l_s[...] = jnp.zeros(l_s.shape, F32)
        acc_s[...] = jnp.zeros(acc_s.shape, F32)

    @pl.when(ki <= qi)
    def _():
        s = lax.dot_general(qhat_s[...], k_ref[...].astype(BF16), NT_DIMS, preferred_element_type=F32)
        r = lax.broadcasted_iota(jnp.int32, (2 * tq, tk), 0)
        c = lax.broadcasted_iota(jnp.int32, (2 * tq, tk), 1)
        q_pos = qi * tq + jnp.where(r >= tq, r - tq, r)
        s = jnp.where(ki * tk + c <= q_pos, s, MASK_NEG)
        m_old = m_s[...]
        m_new = jnp.maximum(m_old, jnp.max(s, axis=-1, keepdims=True))
        alpha = jnp.exp(m_old - m_new)
        p = jnp.exp(s - m_new)
        l_s[...] = alpha * l_s[...] + jnp.sum(p, axis=-1, keepdims=True)
        acc_s[...] = alpha * acc_s[...] + jnp.dot(p.astype(BF16), v_ref[...].astype(BF16),
                                                  preferred_element_type=F32)
        m_s[...] = m_new

    @pl.when(ki == qi)
    def _():
        out = acc_s[...] / l_s[...]
        o = out[0:tq] - lam_ref[...] * out[tq:2 * tq]
        o_ref[...] = _rms(o, sub_ref[...]) * out_scale


def _flash_diff_attention(q, k, v, n_batch, seq_len, lam, subln, out_scale):
    assert seq_len % ATT_TILE == 0
    nt = seq_len // ATT_TILE
    qmap = lambda b, h, qi, ki: (b * nt + qi, h)
    kmap = lambda b, h, qi, ki: (b * nt + jnp.minimum(ki, qi), h)
    const = lambda b, h, qi, ki: (0, 0)
    kern = functools.partial(_flash_kernel, out_scale=out_scale)
    return pl.pallas_call(
        kern,
        grid=(n_batch, A_HEADS, nt, nt),
        in_specs=[pl.BlockSpec((ATT_TILE, LANES), qmap),
                  pl.BlockSpec((ATT_TILE, LANES), kmap),
                  pl.BlockSpec((ATT_TILE, LANES), kmap),
                  pl.BlockSpec((1, A_VD), const),
                  pl.BlockSpec((1, A_VD), const)],
        out_specs=pl.BlockSpec((ATT_TILE, LANES), qmap),
        out_shape=jax.ShapeDtypeStruct((n_batch * seq_len, A_W), F32),
        scratch_shapes=[pltpu.VMEM((2 * ATT_TILE, LANES), BF16),
                        pltpu.VMEM((2 * ATT_TILE, 1), F32),
                        pltpu.VMEM((2 * ATT_TILE, 1), F32),
                        pltpu.VMEM((2 * ATT_TILE, A_VD), F32)],
        compiler_params=pltpu.CompilerParams(dimension_semantics=("arbitrary",) * 4, vmem_limit_bytes=VMEM_LIMIT),
        name="flash_diff_attention",
    )(q, k, v, subln.reshape(1, A_VD), jnp.full((1, A_VD), lam, F32))


SUBLANES = 8
HGRN_CHUNK = 32
HGRN_BLOCK = 512


def _split3_dot(tri, x):
    hi = x.astype(BF16)
    r1 = x - hi.astype(F32)
    mid = r1.astype(BF16)
    lo = (r1 - mid.astype(F32)).astype(BF16)
    return (jnp.dot(tri, hi, preferred_element_type=F32) + jnp.dot(tri, mid, preferred_element_type=F32)
            + jnp.dot(tri, lo, preferred_element_type=F32))


def _cumsum_rows(x, tri):
    n = x.shape[0]
    if n > SUBLANES:
        return _split3_dot(tri, x)
    row = lax.broadcasted_iota(jnp.int32, x.shape, 0)
    shift = 1
    while shift < n:
        x = x + jnp.where(row >= shift, pltpu.roll(x, shift, 0), 0.0)
        shift *= 2
    return x


def _hgrn2_kernel(*refs, chunk, has_s0):
    if has_s0:
        q_ref, f_ref, i_ref, g_ref, lb_ref, nw_ref, s0_ref, o_ref, sout_ref, st_s = refs
    else:
        q_ref, f_ref, i_ref, g_ref, lb_ref, nw_ref, o_ref, sout_ref, st_s = refs
    tb = pl.program_id(2)
    n_chunks = q_ref.shape[0] // chunk
    n_tiles = chunk // SUBLANES

    @pl.when(tb == 0)
    def _():
        if has_s0:
            st_s[...] = s0_ref[...].T
        else:
            st_s[...] = jnp.zeros(st_s.shape, F32)

    lb = lb_ref[...]
    log_lb = jnp.log(jnp.maximum(lb, LB_FLOOR))
    log_1m_lb = jnp.log1p(-lb)
    one_m_lb = 1.0 - lb
    nw = nw_ref[...]
    tri = (lax.broadcasted_iota(jnp.int32, (chunk, chunk), 0)
           >= lax.broadcasted_iota(jnp.int32, (chunk, chunk), 1)).astype(BF16)
    row8 = lax.broadcasted_iota(jnp.int32, (SUBLANES, B_DK), 0)
    mxu = (lambda a: a.astype(BF16)) if chunk >= 2 * SUBLANES else (lambda a: a)

    def body(c, carry):
        r0 = pl.multiple_of(c * chunk, chunk)
        z = f_ref[pl.ds(r0, chunk), :]
        qq = q_ref[pl.ds(r0, chunk), :]
        vv = i_ref[pl.ds(r0, chunk), :]
        gg = g_ref[pl.ds(r0, chunk), :]
        log_sig = jnp.minimum(z, 0.0) - jnp.log1p(jnp.exp(-jnp.abs(z)))
        arg = log_1m_lb + log_sig
        log_f = jnp.maximum(log_lb, arg) + jnp.log1p(jnp.exp(-jnp.abs(log_lb - arg)))
        kk = one_m_lb * jax.nn.sigmoid(-z)
        qf = qq * jax.nn.sigmoid(qq)
        gcum = _cumsum_rows(log_f, tri)
        g_last = gcum[chunk - 1:chunk, :]
        st = st_s[...]
        o_inter = lax.dot_general(mxu(qf * jnp.exp(gcum)), mxu(st), NT_DIMS, preferred_element_type=F32)
        g_t = [gcum[i * SUBLANES:(i + 1) * SUBLANES] for i in range(n_tiles)]
        q_t = [qf[i * SUBLANES:(i + 1) * SUBLANES] for i in range(n_tiles)]
        o_t = [None] * n_tiles
        for s in range(chunk):
            g_s = gcum[s:s + 1, :]
            k_s = kk[s:s + 1, :]
            v_s = vv[s:s + 1, :]
            for ti in range(s // SUBLANES, n_tiles):
                e = jnp.exp(jnp.minimum(g_t[ti] - g_s, 0.0))
                col = jnp.sum(q_t[ti] * (k_s * e), axis=-1, keepdims=True)
                if ti == s // SUBLANES:
                    col = jnp.where(row8[:, 0:1] >= s % SUBLANES, col, 0.0)
                term = col * v_s
                o_t[ti] = term if o_t[ti] is None else o_t[ti] + term
        ob = o_inter + jnp.concatenate(o_t, axis=0)
        kw = kk * jnp.exp(g_last - gcum)
        st_s[...] = st * jnp.exp(g_last) + lax.dot_general(mxu(vv), mxu(kw), TN_DIMS, preferred_element_type=F32)
        o_ref[pl.ds(r0, chunk), :] = _rms(ob, nw) * jax.nn.sigmoid(gg)
        return carry

    lax.fori_loop(0, n_chunks, body, 0)

    @pl.when(tb == pl.num_programs(2) - 1)
    def _():
        sout_ref[...] = st_s[...].T


def _hgrn2(bq, bf, bi, bg, lb, b_norm_w, n_batch, seq_len, row_off, s0, layer_e):
    blk = min(HGRN_BLOCK, seq_len)
    chunk = math.gcd(blk, HGRN_CHUNK)
    assert seq_len % blk == 0 and row_off % blk == 0
    nb = seq_len // blk
    off = row_off // blk
    rmap = lambda b, h, t: (off + b * nb + t, h)
    omap = lambda b, h, t: (b * nb + t, h)
    hmap = lambda b, h, t: (0, h)
    const = lambda b, h, t: (0, 0)
    in_specs = [pl.BlockSpec((blk, B_DK), rmap)] * 4 + [pl.BlockSpec((1, B_DK), hmap), pl.BlockSpec((1, B_DV), const)]
    args = [bq, bf, bi, bg, lb.reshape(1, B_KW), b_norm_w.reshape(1, B_DV)]
    if s0 is not None:
        in_specs.append(pl.BlockSpec((None, None, None, B_DK, B_DV), lambda b, h, t: (layer_e, b, h, 0, 0)))
        args.append(s0)
    kern = functools.partial(_hgrn2_kernel, chunk=chunk, has_s0=s0 is not None)
    return pl.pallas_call(
        kern,
        grid=(n_batch, B_HEADS, nb),
        in_specs=in_specs,
        out_specs=[pl.BlockSpec((blk, B_DV), omap),
                   pl.BlockSpec((None, None, B_DK, B_DV), lambda b, h, t: (b, h, 0, 0))],
        out_shape=[jax.ShapeDtypeStruct((n_batch * seq_len, B_W), F32),
                   jax.ShapeDtypeStruct((n_batch, B_HEADS, B_DK, B_DV), F32)],
        scratch_shapes=[pltpu.VMEM((B_DV, B_DK), F32)],
        compiler_params=pltpu.CompilerParams(dimension_semantics=("arbitrary",) * 3, vmem_limit_bytes=VMEM_LIMIT),
        name="hgrn2",
    )(*args)


SSD_CHUNK = 128
C_PAIRS = C_HEADS // 2
C_GROUP_W = C_DINNER // C_GROUPS


def _softplus(x):
    return jnp.maximum(x, 0.0) + jnp.log1p(jnp.exp(-jnp.abs(x)))


def _ssd_kernel(xbc_ref, z_ref, dt_ref, cw_ref, cb_ref, dtb_ref, alog_ref, cd_ref, nw_ref,
                y_ref, sout_ref, cout_ref, xpad_s, st_s):
    c = pl.program_id(1)
    L = xbc_ref.shape[0]
    tail = C_CONV - 1

    @pl.when(c == 0)
    def _():
        xpad_s[0:SUBLANES, :] = jnp.zeros((SUBLANES, C_CONV_DIM), F32)
        st_s[...] = jnp.zeros(st_s.shape, F32)

    xpad_s[SUBLANES:SUBLANES + L, :] = xbc_ref[...]
    acc = cb_ref[...] + cw_ref[0:1, :] * xpad_s[pl.ds(SUBLANES - tail, L), :]
    for j in range(1, C_CONV):
        acc = acc + cw_ref[j:j + 1, :] * xpad_s[pl.ds(SUBLANES - tail + j, L), :]
    xc = acc * jax.nn.sigmoid(acc)

    @pl.when(c == pl.num_programs(1) - 1)
    def _():
        cout_ref[...] = xpad_s[pl.ds(SUBLANES + L - tail, tail), :]

    xpad_s[0:SUBLANES, :] = xpad_s[L:L + SUBLANES, :]

    dtv = _softplus(dt_ref[...] + dtb_ref[...])
    a = dtv * (-jnp.exp(alog_ref[...]))
    row = lax.broadcasted_iota(jnp.int32, (L, L), 0)
    col = lax.broadcasted_iota(jnp.int32, (L, L), 1)
    causal = row >= col
    gcum = _split3_dot(causal.astype(BF16), a)
    gcum_t = gcum.T
    lane = lax.broadcasted_iota(jnp.int32, (L, LANES), 1)
    low = lane < C_HEADDIM
    srow = lax.broadcasted_iota(jnp.int32, (2 * C_HEADDIM, C_DSTATE), 0)

    for g in range(C_GROUPS):
        b_g = xc[:, C_DINNER + g * C_DSTATE:C_DINNER + (g + 1) * C_DSTATE].astype(BF16)
        c_g = xc[:, C_DINNER + (C_GROUPS + g) * C_DSTATE:C_DINNER + (C_GROUPS + g + 1) * C_DSTATE].astype(BF16)
        cb = lax.dot_general(c_g, b_g, NT_DIMS, preferred_element_type=F32)
        u_parts = []
        sq = None
        for jp in range(C_PAIRS // C_GROUPS):
            pair = g * (C_PAIRS // C_GROUPS) + jp
            x_pair = xc[:, pair * LANES:(pair + 1) * LANES]
            y_pair = None
            gb = []
            dtb = []
            for hh in range(2):
                h = 2 * pair + hh
                g_col = jnp.broadcast_to(gcum[:, h:h + 1], (L, LANES))
                dt_col = jnp.broadcast_to(dtv[:, h:h + 1], (L, LANES))
                gb.append(g_col)
                dtb.append(dt_col)
                dec = jnp.exp(jnp.minimum(g_col - gcum_t[h:h + 1, :], 0.0))
                m_h = jnp.where(causal, cb * dec, 0.0).astype(BF16)
                half = low if hh == 0 else jnp.logical_not(low)
                xdt = jnp.where(half, x_pair * dt_col, 0.0).astype(BF16)
                d = jnp.dot(m_h, xdt, preferred_element_type=F32)
                y_pair = d if y_pair is None else y_pair + d
            st = st_s[pair]
            y_inter = lax.dot_general(c_g, st.astype(BF16), NT_DIMS, preferred_element_type=F32)
            y_pair = y_pair + y_inter * jnp.where(low, jnp.exp(gb[0]), jnp.exp(gb[1]))
            y_pair = y_pair + cd_ref[:, pair * LANES:(pair + 1) * LANES] * x_pair
            gl = [gb[0][L - 1:L, :], gb[1][L - 1:L, :]]
            w_pair = jnp.where(low, jnp.exp(gl[0] - gb[0]) * dtb[0], jnp.exp(gl[1] - gb[1]) * dtb[1])
            decay = jnp.where(srow < C_HEADDIM, jnp.exp(gl[0]), jnp.exp(gl[1]))
            st_s[pair] = decay * st + lax.dot_general((x_pair * w_pair).astype(BF16), b_g, TN_DIMS,
                                                      preferred_element_type=F32)
            zz = z_ref[:, pair * LANES:(pair + 1) * LANES]
            u = y_pair * (zz * jax.nn.sigmoid(zz))
            u_parts.append(u)
            s2 = jnp.sum(u * u, axis=-1, keepdims=True)
            sq = s2 if sq is None else sq + s2
        inv = lax.rsqrt(sq * (1.0 / C_GROUP_W) + EPS)
        for jp, u in enumerate(u_parts):
            pair = g * (C_PAIRS // C_GROUPS) + jp
            y_ref[:, pair * LANES:(pair + 1) * LANES] = u * inv * nw_ref[:, pair * LANES:(pair + 1) * LANES]

    @pl.when(c == pl.num_programs(1) - 1)
    def _():
        sout_ref[...] = st_s[...]


def _ssd_prompt(xbc, z, dt, conv_w, conv_b, dt_bias, a_log, c_d, c_norm_w, n_batch, seq_len):
    assert seq_len % SSD_CHUNK == 0
    nc = seq_len // SSD_CHUNK
    rmap = lambda b, c: (b * nc + c, 0)
    const = lambda b, c: (0, 0)
    pad16 = lambda v: jnp.pad(v.astype(F32), (0, DT_PAD - C_HEADS)).reshape(1, DT_PAD)
    y, s_out, c_out = pl.pallas_call(
        _ssd_kernel,
        grid=(n_batch, nc),
        in_specs=[pl.BlockSpec((SSD_CHUNK, C_CONV_DIM), rmap),
                  pl.BlockSpec((SSD_CHUNK, C_DINNER), rmap),
                  pl.BlockSpec((SSD_CHUNK, DT_PAD), rmap),
                  pl.BlockSpec((C_CONV, C_CONV_DIM), const),
                  pl.BlockSpec((1, C_CONV_DIM), const),
                  pl.BlockSpec((1, DT_PAD), const),
                  pl.BlockSpec((1, DT_PAD), const),
                  pl.BlockSpec((1, C_DINNER), const),
                  pl.BlockSpec((1, C_DINNER), const)],
        out_specs=[pl.BlockSpec((SSD_CHUNK, C_DINNER), rmap),
                   pl.BlockSpec((None, C_PAIRS, 2 * C_HEADDIM, C_DSTATE), lambda b, c: (b, 0, 0, 0)),
                   pl.BlockSpec((None, C_CONV - 1, C_CONV_DIM), lambda b, c: (b, 0, 0))],
        out_shape=[jax.ShapeDtypeStruct((n_batch * seq_len, C_DINNER), F32),
                   jax.ShapeDtypeStruct((n_batch, C_PAIRS, 2 * C_HEADDIM, C_DSTATE), F32),
                   jax.ShapeDtypeStruct((n_batch, C_CONV - 1, C_CONV_DIM), F32)],
        scratch_shapes=[pltpu.VMEM((SUBLANES + SSD_CHUNK, C_CONV_DIM), F32),
                        pltpu.VMEM((C_PAIRS, 2 * C_HEADDIM, C_DSTATE), F32)],
        compiler_params=pltpu.CompilerParams(dimension_semantics=("arbitrary", "arbitrary"),
                                             vmem_limit_bytes=VMEM_LIMIT),
        name="ssd_prompt",
    )(xbc, z, dt, conv_w, conv_b.reshape(1, C_CONV_DIM), pad16(dt_bias), pad16(a_log),
      jnp.repeat(c_d.astype(F32), C_HEADDIM).reshape(1, C_DINNER), c_norm_w.reshape(1, C_DINNER))
    return y, s_out.reshape(n_batch, C_HEADS, C_HEADDIM, C_DSTATE), c_out


def _rmsnorm_j(x, g):
    xf = x.astype(F32)
    r = xf * lax.rsqrt(jnp.mean(xf * xf, axis=-1, keepdims=True) + EPS)
    return (r * g.astype(F32)).astype(x.dtype)


def _to_chunks(a, L):
    b, t = a.shape[:2]
    return jnp.moveaxis(a.reshape((b, t // L, L) + a.shape[2:]), 1, 0)


def _from_chunks(a):
    n, b, L = a.shape[:3]
    return jnp.moveaxis(a, 0, 1).reshape((b, n * L) + a.shape[3:])


def _diff_lambda(lam, layer):
    lam = lam.astype(F32)
    lam_init = 0.8 - 0.6 * math.exp(-0.3 * layer)
    lam_full = jnp.exp(jnp.sum(lam[0] * lam[1])) - jnp.exp(jnp.sum(lam[2] * lam[3])) + lam_init
    return lam_full, lam_init


def _diff_attend(q, parts, lam):
    scale = A_QK ** -0.5
    scores = []
    for k, _, m in parts:
        s = jnp.einsum("bqhcd,bkhcd->bhcqk", q, k).astype(F32) * scale
        scores.append(s if m is None else jnp.where(m, s, MASK_NEG))
    p = jax.nn.softmax(jnp.concatenate(scores, axis=-1), axis=-1)
    p = p[:, :, 0] - lam * p[:, :, 1]
    out = None
    off = 0
    for k, v, _ in parts:
        n = k.shape[1]
        o = jnp.einsum("bhqk,bkhd->bqhd", p[..., off:off + n].astype(v.dtype), v)
        out = o if out is None else out + o
        off += n
    return out


def _diff_attn_prompt(q, k, v, lam):
    b, t = q.shape[:2]
    nb = t // A_Q_BLOCK
    qb = jnp.moveaxis(q.reshape(b, nb, A_Q_BLOCK, A_HEADS, 2, A_QK), 1, 0)
    k_pos = jnp.arange(t)

    def block(args):
        qi, i = args
        q_pos = i * A_Q_BLOCK + jnp.arange(A_Q_BLOCK)
        mask = k_pos[None, :] <= q_pos[:, None]
        return _diff_attend(qi, [(k, v, mask)], lam)

    o = lax.map(block, (qb, jnp.arange(nb)))
    return jnp.moveaxis(o, 0, 1).reshape(b, t, A_HEADS, A_VD)


def _diff_attn_sample(q, k, v, k_past, v_past, lam):
    t = q.shape[1]
    causal = jnp.tril(jnp.ones((t, t), bool))
    return _diff_attend(q, [(k_past, v_past, None), (k, v, causal)], lam)


def _hgrn_lower_bound(p):
    sm = jax.nn.softmax(p.astype(F32), axis=0)
    return jnp.concatenate([jnp.zeros_like(sm[:1]), jnp.cumsum(sm[1:], axis=0)], axis=0)


def _hgrn2_scan(q, f_logit, i, lb, S0):
    t = q.shape[1]
    L = math.gcd(t, B_CHUNK)
    z = f_logit.astype(F32)
    log_lb = jnp.log(jnp.maximum(lb, LB_FLOOR))
    log_f = jnp.logaddexp(log_lb, jnp.log1p(-lb) + jax.nn.log_sigmoid(z))
    k = (1.0 - lb) * jax.nn.sigmoid(-z)
    qf = jax.nn.silu(q.astype(F32))
    vf = i.astype(F32)
    causal = jnp.tril(jnp.ones((L, L), bool))

    def step(S, c):
        qc, kc, vc, gc = c
        G = jnp.cumsum(gc, axis=1)
        o_inter = jnp.einsum("blhk,bhkv->blhv", qc * jnp.exp(G), S)
        rel = jnp.where(causal[None, :, :, None, None], G[:, :, None] - G[:, None], MASK_NEG)
        A = jnp.einsum("bthk,bshk,btshk->bhts", qc, kc, jnp.exp(rel))
        o = o_inter + jnp.einsum("bhts,bshv->bthv", A, vc)
        GL = G[:, -1]
        S = jnp.exp(GL)[..., None] * S + jnp.einsum("bshk,bshv->bhkv", kc * jnp.exp(GL[:, None] - G), vc)
        return S, o

    S, o = lax.scan(step, S0.astype(F32), (_to_chunks(qf, L), _to_chunks(k, L), _to_chunks(vf, L), _to_chunks(log_f, L)))
    return _from_chunks(o), S


def _causal_conv(xbc, conv0, w, b):
    xp = jnp.concatenate([conv0, xbc], axis=1)
    y = lax.conv_general_dilated(xp, w.astype(xp.dtype)[:, None, :], window_strides=(1,), padding="VALID",
                                 dimension_numbers=("NWC", "WIO", "NWC"), feature_group_count=xp.shape[-1])
    return jax.nn.silu(y + b.astype(xp.dtype)), xp[:, -(C_CONV - 1):]


def _ssd_scan(x, dt, Bm, Cm, A, S0):
    t = x.shape[1]
    L = math.gcd(t, C_CHUNK)
    a = dt * A
    causal = jnp.tril(jnp.ones((L, L), bool))

    def step(S, c):
        xc, dtc, ac, Bc, Cc = c
        G = jnp.cumsum(ac, axis=1)
        rel = jnp.where(causal[None, :, :, None], G[:, :, None] - G[:, None], MASK_NEG)
        M = jnp.einsum("bthn,bshn->btsh", Cc, Bc) * jnp.exp(rel) * dtc[:, None]
        y = jnp.einsum("btsh,bshp->bthp", M, xc) + jnp.einsum("bthn,bhpn->bthp", Cc, S) * jnp.exp(G)[..., None]
        GL = G[:, -1]
        S = jnp.exp(GL)[:, :, None, None] * S + jnp.einsum("bsh,bshp,bshn->bhpn", jnp.exp(GL[:, None] - G) * dtc, xc, Bc)
        return S, y

    S, y = lax.scan(step, S0.astype(F32),
                    (_to_chunks(x, L), _to_chunks(dt, L), _to_chunks(a, L), _to_chunks(Bm, L), _to_chunks(Cm, L)))
    return _from_chunks(y), S


def _gated_rmsnorm(y, z, g):
    b, t, _ = y.shape
    u = (y * jax.nn.silu(z.astype(F32))).reshape(b, t, C_GROUPS, C_DINNER // C_GROUPS)
    u = u * lax.rsqrt(jnp.mean(u * u, axis=-1, keepdims=True) + EPS)
    return u.reshape(b, t, C_DINNER) * g.astype(F32)


def _window_attend(q, k, v, mask, sinks):
    b, tq = q.shape[:2]
    qg = q.reshape(b, tq, D_KV_HEADS, D_GROUP, D_HD)
    s = jnp.einsum("bqgrd,bkgd->bgrqk", qg, k).astype(F32) * (D_HD ** -0.5)
    s = jnp.where(mask, s, MASK_NEG)
    sink = sinks.astype(F32).reshape(1, D_KV_HEADS, D_GROUP, 1, 1)
    m = jnp.maximum(jnp.max(s, axis=-1, keepdims=True), sink)
    p = jnp.exp(s - m)
    p = p / (jnp.sum(p, axis=-1, keepdims=True) + jnp.exp(sink - m))
    o = jnp.einsum("bgrqk,bkgd->bqgrd", p.astype(v.dtype), v)
    return o.reshape(b, tq, D_HEADS, D_HD)


def _swa_prompt(q, k, v, sinks):
    b, t = q.shape[:2]
    nb = t // WINDOW
    kb = k.reshape(b, nb, WINDOW, D_KV_HEADS, D_HD)
    vb = v.reshape(b, nb, WINDOW, D_KV_HEADS, D_HD)
    kk = jnp.concatenate([jnp.concatenate([jnp.zeros_like(kb[:, :1]), kb[:, :-1]], axis=1), kb], axis=2)
    vv = jnp.concatenate([jnp.concatenate([jnp.zeros_like(vb[:, :1]), vb[:, :-1]], axis=1), vb], axis=2)
    q_rel = WINDOW + jnp.arange(WINDOW)
    k_rel = jnp.arange(2 * WINDOW)
    band = (k_rel[None, :] <= q_rel[:, None]) & (k_rel[None, :] > q_rel[:, None] - WINDOW)
    valid = (jnp.arange(nb)[:, None] > 0) | (k_rel[None, :] >= WINDOW)
    mask = band[None] & valid[:, None, :]
    o = _window_attend(q.reshape(b * nb, WINDOW, D_HEADS, D_HD), kk.reshape(b * nb, 2 * WINDOW, D_KV_HEADS, D_HD),
                       vv.reshape(b * nb, 2 * WINDOW, D_KV_HEADS, D_HD), jnp.tile(mask, (b, 1, 1))[:, None, None], sinks)
    return o.reshape(b, t, D_HEADS, D_HD)


def _swa_sample(q, k, v, kc, vc, sinks):
    t = q.shape[1]
    w0 = kc.shape[1]
    kk = jnp.concatenate([kc.astype(k.dtype), k], axis=1)
    vv = jnp.concatenate([vc.astype(v.dtype), v], axis=1)
    q_rel = w0 + jnp.arange(t)
    k_rel = jnp.arange(w0 + t)
    mask = (k_rel[None, :] <= q_rel[:, None]) & (k_rel[None, :] > q_rel[:, None] - WINDOW)
    o = _window_attend(q, kk, vv, mask, sinks)
    return o, kk[:, -WINDOW:], vv[:, -WINDOW:]


def _even_mixer(parts, b, t, layer, e, w, past):
    aq, ak, av, bq, bf, bi, bg = parts
    q = aq.reshape(b, t, A_HEADS, 2, A_QK)
    k = ak.reshape(b, t, A_HEADS, 2, A_QK)
    v = av.reshape(b, t, A_HEADS, A_VD)
    lam, lam_init = _diff_lambda(w["a_lambda"][e], layer)
    if past is None:
        o = _diff_attn_prompt(q, k, v, lam)
        S0 = jnp.zeros((b, B_HEADS, B_DK, B_DV), F32)
    else:
        pt = past["page_table"]
        p_len = pt.shape[1] * PAGE_SIZE
        k_past = past["a_k"][e, pt].reshape(b, p_len, A_HEADS, 2, A_QK)
        v_past = past["a_v"][e, pt].reshape(b, p_len, A_HEADS, A_VD)
        o = _diff_attn_sample(q, k, v, k_past, v_past, lam)
        S0 = past["b_state"][e]
    a_out = _rmsnorm_j(o, w["a_subln"][e]) * (1.0 - lam_init)
    lb = _hgrn_lower_bound(w["b_lower_bounds"])[e].reshape(B_HEADS, B_DK)
    ob, S_new = _hgrn2_scan(bq.reshape(b, t, B_HEADS, B_DK), bf.reshape(b, t, B_HEADS, B_DK),
                            bi.reshape(b, t, B_HEADS, B_DV), lb, S0)
    b_out = _rmsnorm_j(ob, w["b_norm"][e]) * jax.nn.sigmoid(bg.reshape(b, t, B_HEADS, B_DV))
    mix = jnp.concatenate([a_out.reshape(b * t, A_W), b_out.reshape(b * t, B_W)], axis=-1)
    return mix, (k.reshape(b, t, A_HEADS, 2 * A_QK), v, S_new)


def _odd_mixer(parts, b, t, e, w, past):
    z, xbc, dt, dq, dk, dv = parts
    z = z.reshape(b, t, C_DINNER)
    xbc = xbc.reshape(b, t, C_CONV_DIM)
    dt = dt[:, :C_HEADS].reshape(b, t, C_HEADS)
    if past is None:
        conv0 = jnp.zeros((b, C_CONV - 1, C_CONV_DIM), xbc.dtype)
        S0 = jnp.zeros((b, C_HEADS, C_HEADDIM, C_DSTATE), F32)
    else:
        conv0 = past["c_conv"][e]
        S0 = past["c_ssm"][e]
    xbc, conv_new = _causal_conv(xbc, conv0, w["c_conv_w"][e], w["c_conv_b"][e])
    xs, Bm, Cm = jnp.split(xbc, [C_DINNER, C_DINNER + C_GROUPS * C_DSTATE], axis=-1)
    rep = C_HEADS // C_GROUPS
    xs = xs.reshape(b, t, C_HEADS, C_HEADDIM)
    Bm = jnp.repeat(Bm.reshape(b, t, C_GROUPS, C_DSTATE), rep, axis=2)
    Cm = jnp.repeat(Cm.reshape(b, t, C_GROUPS, C_DSTATE), rep, axis=2)
    dt = jax.nn.softplus(dt + w["c_dt_bias"][e])
    A = -jnp.exp(w["c_a_log"][e])
    y, S_new = _ssd_scan(xs, dt, Bm, Cm, A, S0)
    y = y + w["c_d"][e][:, None] * xs
    y = _gated_rmsnorm(y.reshape(b, t, C_DINNER), z, w["c_norm"][e])
    q = dq.reshape(b, t, D_HEADS, D_HD)
    k = dk.reshape(b, t, D_KV_HEADS, D_HD)
    v = dv.reshape(b, t, D_KV_HEADS, D_HD)
    if past is None:
        o = _swa_prompt(q, k, v, w["d_sinks"][e])
        k_win, v_win = k[:, -WINDOW:], v[:, -WINDOW:]
    else:
        o, k_win, v_win = _swa_sample(q, k, v, past["d_k"][e], past["d_v"][e], w["d_sinks"][e])
    mix = jnp.concatenate([y.reshape(b * t, C_DINNER), o.reshape(b * t, D_HEADS * D_HD)], axis=-1)
    return mix, (S_new, conv_new, k_win, v_win)


def _rope_tables(pos):
    inv = ROPE_THETA ** (-jnp.arange(0, A_QK, 2, dtype=F32) / A_QK)
    ang = pos.astype(F32)[:, None] * inv[None, :]
    cos = jnp.tile(jnp.cos(ang), (1, LANES // ROPE_HALF))
    sin = jnp.tile(jnp.sin(ang), (1, LANES // ROPE_HALF))
    first_half = (jnp.arange(LANES) % (2 * ROPE_HALF)) < ROPE_HALF
    sin_lo = jnp.where(first_half[None, :], -sin, 0.0)
    sin_hi = jnp.where(first_half[None, :], 0.0, sin)
    return cos, sin_lo, sin_hi


def kernel(x_prompt, x_sample, cache_a_k, cache_a_v, state_b, state_c_ssm, state_c_conv, cache_d_k, cache_d_v,
           page_table, norm_w, mlp_w1, mlp_w2, w_in_e, w_out_e, a_lambda, a_subln, b_lower_bounds, b_norm,
           w_in_o, w_out_o, c_conv_w, c_conv_b, c_dt_bias, c_a_log, c_d, c_norm, d_sinks):
    w = {"a_lambda": a_lambda, "a_subln": a_subln, "b_lower_bounds": b_lower_bounds, "b_norm": b_norm,
         "c_conv_w": c_conv_w, "c_conv_b": c_conv_b, "c_dt_bias": c_dt_bias, "c_a_log": c_a_log,
         "c_d": c_d, "c_norm": c_norm, "d_sinks": d_sinks}
    bp, tp, _ = x_prompt.shape
    bs, ts, _ = x_sample.shape
    mp, ms = bp * tp, bs * ts
    past_len = page_table.shape[1] * PAGE_SIZE
    pos = jnp.concatenate([jnp.tile(jnp.arange(tp), bp), jnp.tile(past_len + jnp.arange(ts), bs)])
    rope_tabs = _rope_tables(pos)
    past = {"page_table": page_table, "a_k": cache_a_k, "a_v": cache_a_v, "b_state": state_b,
            "c_ssm": state_c_ssm, "c_conv": state_c_conv, "d_k": cache_d_k, "d_v": cache_d_v}

    w_in_e_b = w_in_e.astype(BF16)
    dt_off = C_DINNER + C_CONV_DIM
    w_in_o_b = jnp.concatenate(
        [w_in_o[:, :, :dt_off],
         jnp.pad(w_in_o[:, :, dt_off:dt_off + C_HEADS], ((0, 0), (0, 0), (0, DT_PAD - C_HEADS))),
         w_in_o[:, :, dt_off + C_HEADS:]], axis=-1).astype(BF16)
    n_pool = cache_a_k.shape[1]
    cache_k4 = cache_a_k.reshape(cache_a_k.shape[0], n_pool, PAGE_SIZE, A_QKW)
    cache_v4 = cache_a_v.reshape(cache_a_v.shape[0], n_pool, PAGE_SIZE, A_W)
    w_out_e_b = w_out_e.astype(BF16)
    w_out_o_b = w_out_o.astype(BF16)
    w1_b = mlp_w1.astype(BF16)
    w2_b = mlp_w2.astype(BF16)

    even_widths = (A_QKW, A_QKW, A_W, B_KW, B_KW, B_W, B_W)
    even_rope = (True, True, False, False, False, False, False)
    odd_widths = (C_DINNER, C_CONV_DIM, DT_PAD, D_HEADS * D_HD, D_KV_HEADS * D_HD, D_KV_HEADS * D_HD)
    odd_rope = (False, False, False, True, True, False)

    x = jnp.concatenate([x_prompt.reshape(mp, D_MODEL), x_sample.reshape(ms, D_MODEL)], axis=0)
    st_p = {n: [] for n in ("a_k", "a_v", "b_state", "c_ssm", "c_conv", "d_k", "d_v")}
    st_s = {n: [] for n in st_p}
    for l in range(DEPTH):
        g = norm_w[l]
        e = l // 2
        if l % 2 == 0:
            aq, ak, av, bq, bf, bi, bg = _in_projection(x, g[0:1], w_in_e_b[e], rope_tabs, even_widths, even_rope)
            lam, lam_init = _diff_lambda(a_lambda[e], l)
            out_scale = 1.0 - lam_init
            a_p = _flash_diff_attention(aq, ak, av, bp, tp, lam, a_subln[e], out_scale)
            shp = (bs, ts, A_QKW)
            a_s = _paged_diff_attention(aq[mp:].reshape(shp), ak[mp:].reshape(shp), av[mp:].reshape(shp),
                                        cache_k4, cache_v4, page_table, e, lam, a_subln[e], out_scale)
            lb = _hgrn_lower_bound(b_lower_bounds)[e]
            b_p, bst_p = _hgrn2(bq, bf, bi, bg, lb, b_norm[e], bp, tp, 0, None, e)
            b_s, bst_s = _hgrn2(bq, bf, bi, bg, lb, b_norm[e], bs, ts, mp, state_b, e)
            mix_p = jnp.concatenate([a_p, b_p], axis=-1)
            mix_s = jnp.concatenate([a_s.reshape(ms, A_W), b_s], axis=-1)
            st_p["a_k"].append(ak[:mp].reshape(bp, tp, A_HEADS, 2 * A_QK))
            st_p["a_v"].append(av[:mp].reshape(bp, tp, A_HEADS, A_VD))
            st_p["b_state"].append(bst_p)
            st_s["a_k"].append(ak[mp:].reshape(bs, ts, A_HEADS, 2 * A_QK))
            st_s["a_v"].append(av[mp:].reshape(bs, ts, A_HEADS, A_VD))
            st_s["b_state"].append(bst_s)
            w_out = w_out_e_b[e]
        else:
            parts = _in_projection(x, g[0:1], w_in_o_b[e], rope_tabs, odd_widths, odd_rope)
            z, xbc, dt, dq, dk, dv = parts
            y_p, ssm_p, conv_p = _ssd_prompt(xbc, z, dt, c_conv_w[e], c_conv_b[e], c_dt_bias[e], c_a_log[e],
                                             c_d[e], c_norm[e], bp, tp)
            k_p = dk[:mp].reshape(bp, tp, D_KV_HEADS, D_HD)
            v_p = dv[:mp].reshape(bp, tp, D_KV_HEADS, D_HD)
            o_p = _swa_prompt(dq[:mp].reshape(bp, tp, D_HEADS, D_HD), k_p, v_p, d_sinks[e])
            mix_p = jnp.concatenate([y_p, o_p.reshape(mp, D_HEADS * D_HD)], axis=-1)
            vals_p = (ssm_p, conv_p, k_p[:, -WINDOW:], v_p[:, -WINDOW:])
            mix_s, vals_s = _odd_mixer([p[mp:] for p in parts], bs, ts, e, w, past)
            for d, vals in ((st_p, vals_p), (st_s, vals_s)):
                d["c_ssm"].append(vals[0])
                d["c_conv"].append(vals[1])
                d["d_k"].append(vals[2])
                d["d_v"].append(vals[3])
            w_out = w_out_o_b[e]
        mix = jnp.concatenate([mix_p, mix_s], axis=0)
        x = _post_mixer(mix, x, w_out, g, w1_b[l], w2_b[l])
    sp = {n: jnp.stack(v, axis=0) for n, v in st_p.items()}
    ss = {n: jnp.stack(v, axis=0) for n, v in st_s.items()}
    y_prompt = x[:mp].reshape(bp, tp, D_MODEL)
    y_sample = x[mp:].reshape(bs, ts, D_MODEL)
    return (y_prompt, y_sample,
            sp["a_k"], ss["a_k"], sp["a_v"], ss["a_v"],
            sp["b_state"], ss["b_state"], sp["c_ssm"], ss["c_ssm"],
            sp["c_conv"], ss["c_conv"], sp["d_k"], ss["d_k"], sp["d_v"], ss["d_v"])
```

```python
import functools
import math

import jax
import jax.numpy as jnp
import numpy as np
from jax import lax
from jax.experimental import pallas as pl
from jax.experimental.pallas import tpu as pltpu

F32 = jnp.float32
BF16 = jnp.bfloat16

D_MODEL = 1024
DEPTH = 4
PAGE_SIZE = 128
D_FF = 4 * D_MODEL
EPS = 1e-6
ROPE_THETA = 10000.0
MASK_NEG = -1e30
LB_FLOOR = 1e-30

A_HEADS = D_MODEL // 256
A_QK = 64
A_VD = 2 * A_QK
A_Q_BLOCK = 128
A_QKW = A_HEADS * 2 * A_QK
A_W = A_HEADS * A_VD
B_HEADS = 4
B_DK = D_MODEL // 8
B_DV = D_MODEL // 8
B_CHUNK = 64
B_KW = B_HEADS * B_DK
B_W = B_HEADS * B_DV
C_HEADS = 16
C_HEADDIM = 64
C_DINNER = C_HEADS * C_HEADDIM
C_GROUPS = 2
C_DSTATE = 128
C_CONV = 4
C_CHUNK = 128
C_CONV_DIM = C_DINNER + 2 * C_GROUPS * C_DSTATE
D_HEADS = 8
D_KV_HEADS = 2
D_GROUP = D_HEADS // D_KV_HEADS
D_HD = 64
WINDOW = 128

LANES = 128
ROPE_HALF = 32
DT_PAD = LANES
VMEM_LIMIT = 56 * 1024 * 1024
ROW_TILE = 512


def _rms(x, g):
    return x * lax.rsqrt(jnp.mean(x * x, axis=-1, keepdims=True) + EPS) * g


def _rope_tile(y, cos, sin_lo, sin_hi):
    up = pltpu.roll(y, LANES - ROPE_HALF, 1)
    down = pltpu.roll(y, ROPE_HALF, 1)
    return y * cos + up * sin_lo + down * sin_hi


def _proj_kernel(x_ref, g_ref, w_ref, cos_ref, slo_ref, shi_ref, *out_refs, widths, rope_flags):
    h = _rms(x_ref[...], g_ref[...]).astype(BF16)
    cos = cos_ref[...]
    slo = slo_ref[...]
    shi = shi_ref[...]
    off = 0
    for o_ref, width, roped in zip(out_refs, widths, rope_flags):
        y = jnp.dot(h, w_ref[:, off:off + width], preferred_element_type=F32)
        if roped:
            for c in range(0, width, LANES):
                o_ref[:, c:c + LANES] = _rope_tile(y[:, c:c + LANES], cos, slo, shi)
        else:
            o_ref[...] = y
        off += width


def _in_projection(x, g, w, rope_tabs, widths, rope_flags):
    m = x.shape[0]
    n = w.shape[1]
    assert sum(widths) == n and m % ROW_TILE == 0
    row = lambda i: (i, 0)
    const = lambda i: (0, 0)
    kern = functools.partial(_proj_kernel, widths=tuple(widths), rope_flags=tuple(rope_flags))
    return pl.pallas_call(
        kern,
        grid=(m // ROW_TILE,),
        in_specs=[
            pl.BlockSpec((ROW_TILE, D_MODEL), row),
            pl.BlockSpec((1, D_MODEL), const),
            pl.BlockSpec((D_MODEL, n), const),
            pl.BlockSpec((ROW_TILE, LANES), row),
            pl.BlockSpec((ROW_TILE, LANES), row),
            pl.BlockSpec((ROW_TILE, LANES), row),
        ],
        out_specs=[pl.BlockSpec((ROW_TILE, wd), row) for wd in widths],
        out_shape=[jax.ShapeDtypeStruct((m, wd), F32) for wd in widths],
        compiler_params=pltpu.CompilerParams(dimension_semantics=("arbitrary",), vmem_limit_bytes=VMEM_LIMIT),
        name="in_projection",
    )(x, g, w, *rope_tabs)


FF_TILE = 1024


def _post_kernel(*refs, group_sizes, group_tiles):
    n_pieces = sum(group_sizes)
    piece_refs = refs[:n_pieces]
    x_ref, wo_ref, g_ref, w1_ref, w2_ref, o_ref, x1_s, h_s, acc_s = refs[n_pieces:]
    i = pl.program_id(0)
    j = pl.program_id(1)

    def start_tile(pieces):
        m = None
        off = 0
        for p_ref in pieces:
            width = p_ref.shape[1]
            d = jnp.dot(p_ref[...].astype(BF16), wo_ref[off:off + width, :], preferred_element_type=F32)
            m = d if m is None else m + d
            off += width
        x1 = x_ref[...] + _rms(m, g_ref[1:2, :])
        x1_s[...] = x1
        h_s[...] = _rms(x1, g_ref[2:3, :]).astype(BF16)
        acc_s[...] = jnp.zeros_like(acc_s)

    first = 0
    pos = 0
    for size, tiles in zip(group_sizes, group_tiles):
        pieces = piece_refs[pos:pos + size]

        @pl.when((j == 0) & (i >= first) & (i < first + tiles))
        def _(pieces=pieces):
            start_tile(pieces)

        first += tiles
        pos += size

    a = jnp.dot(h_s[...], w1_ref[...], preferred_element_type=F32)
    a = jnp.square(jnp.maximum(a, 0.0)).astype(BF16)
    acc_s[...] += jnp.dot(a, w2_ref[...], preferred_element_type=F32)

    @pl.when(j == pl.num_programs(1) - 1)
    def _():
        o_ref[...] = x1_s[...] + _rms(acc_s[...], g_ref[3:4, :])


def _post_mixer(mix_groups, x, w_out, g, w1, w2):
    m = x.shape[0]
    kmix = w_out.shape[0]
    piece_specs = []
    pieces = []
    first = 0
    group_tiles = []
    for group in mix_groups:
        rows = group[0].shape[0]
        assert rows % ROW_TILE == 0 and sum(p.shape[1] for p in group) == kmix
        tiles = rows // ROW_TILE
        for p in group:
            assert p.shape[0] == rows
            piece_specs.append(pl.BlockSpec(
                (ROW_TILE, p.shape[1]), lambda i, j, first=first, tiles=tiles: (jnp.clip(i - first, 0, tiles - 1), 0)))
            pieces.append(p)
        group_tiles.append(tiles)
        first += tiles
    assert first * ROW_TILE == m
    kern = functools.partial(_post_kernel, group_sizes=tuple(len(grp) for grp in mix_groups),
                             group_tiles=tuple(group_tiles))
    return pl.pallas_call(
        kern,
        grid=(m // ROW_TILE, D_FF // FF_TILE),
        in_specs=piece_specs + [
            pl.BlockSpec((ROW_TILE, D_MODEL), lambda i, j: (i, 0)),
            pl.BlockSpec((kmix, D_MODEL), lambda i, j: (0, 0)),
            pl.BlockSpec((4, D_MODEL), lambda i, j: (0, 0)),
            pl.BlockSpec((D_MODEL, FF_TILE), lambda i, j: (0, j)),
            pl.BlockSpec((FF_TILE, D_MODEL), lambda i, j: (j, 0)),
        ],
        out_specs=pl.BlockSpec((ROW_TILE, D_MODEL), lambda i, j: (i, 0)),
        out_shape=jax.ShapeDtypeStruct((m, D_MODEL), F32),
        scratch_shapes=[
            pltpu.VMEM((ROW_TILE, D_MODEL), F32),
            pltpu.VMEM((ROW_TILE, D_MODEL), BF16),
            pltpu.VMEM((ROW_TILE, D_MODEL), F32),
        ],
        compiler_params=pltpu.CompilerParams(dimension_semantics=("arbitrary", "arbitrary"),
                                             vmem_limit_bytes=VMEM_LIMIT),
        name="post_mixer_mlp",
    )(*pieces, x, w_out, g, w1, w2)


DEC_PAGES = 8
A_ROWS_PER_HEAD = 2
NT_DIMS = (((1,), (1,)), ((), ()))
TN_DIMS = (((0,), (0,)), ((), ()))


def _decode_kernel(pt_ref, q_ref, kn_ref, vn_ref, sub_ref, lam_ref, *rest, n_pages, t_new, out_scale):
    del pt_ref
    k_refs = rest[:n_pages]
    v_refs = rest[n_pages:2 * n_pages]
    o_ref = rest[2 * n_pages]
    qhat_s, m_s, l_s, acc_s = rest[2 * n_pages + 1:]
    j = pl.program_id(1)
    rph = A_ROWS_PER_HEAD * t_new
    rows = A_HEADS * rph

    def page_tokens(ref):
        return jnp.concatenate([ref[pl.ds(h, PAGE_SIZE, stride=A_HEADS), :] for h in range(A_HEADS)],
                               axis=1).astype(BF16)

    def head_diag(pv):
        return jnp.concatenate([pv[h * rph:(h + 1) * rph, h * LANES:(h + 1) * LANES] for h in range(A_HEADS)], axis=0)

    @pl.when(j == 0)
    def _():
        qhat_s[...] = jnp.zeros(qhat_s.shape, BF16)
        qt = q_ref[...] * (A_QK ** -0.5)
        lane = lax.broadcasted_iota(jnp.int32, (t_new, LANES), 1)
        r = lax.broadcasted_iota(jnp.int32, (rph, PAGE_SIZE), 0)
        c = lax.broadcasted_iota(jnp.int32, (rph, PAGE_SIZE), 1)
        pad = jnp.zeros((PAGE_SIZE - t_new, LANES), F32)
        for h in range(A_HEADS):
            qh = qt[:, h * LANES:(h + 1) * LANES]
            qhat = jnp.concatenate([jnp.where(lane < A_QK, qh, 0.0), jnp.where(lane < A_QK, 0.0, qh)],
                                   axis=0).astype(BF16)
            qhat_s[h * rph:(h + 1) * rph, h * LANES:(h + 1) * LANES] = qhat
            kn = jnp.concatenate([kn_ref[:, h * LANES:(h + 1) * LANES], pad], axis=0).astype(BF16)
            vn = jnp.concatenate([vn_ref[:, h * LANES:(h + 1) * LANES], pad], axis=0).astype(BF16)
            s = lax.dot_general(qhat, kn, NT_DIMS, preferred_element_type=F32)
            s = jnp.where(c <= r % t_new, s, MASK_NEG)
            m = jnp.max(s, axis=-1, keepdims=True)
            p = jnp.exp(s - m)
            m_s[h * rph:(h + 1) * rph, :] = m
            l_s[h * rph:(h + 1) * rph, :] = jnp.sum(p, axis=-1, keepdims=True)
            acc_s[h * rph:(h + 1) * rph, :] = jnp.dot(p.astype(BF16), vn, preferred_element_type=F32)

    qhat_all = qhat_s[...]
    ss = [lax.dot_general(qhat_all, page_tokens(k_refs[n]), NT_DIMS, preferred_element_type=F32)
          for n in range(n_pages)]
    smax = functools.reduce(jnp.maximum, ss)
    m_old = m_s[...]
    m_new = jnp.maximum(m_old, jnp.max(smax, axis=-1, keepdims=True))
    alpha = jnp.exp(m_old - m_new)
    ps = [jnp.exp(s - m_new) for s in ss]
    l_s[...] = alpha * l_s[...] + jnp.sum(functools.reduce(jnp.add, ps), axis=-1, keepdims=True)
    pv = None
    for n in range(n_pages):
        d = jnp.dot(ps[n].astype(BF16), page_tokens(v_refs[n]), preferred_element_type=F32)
        pv = d if pv is None else pv + d
    acc_s[...] = alpha * acc_s[...] + head_diag(pv)
    m_s[...] = m_new

    @pl.when(j == pl.num_programs(1) - 1)
    def _():
        out = acc_s[...] / l_s[...]
        lam = lam_ref[...]
        for h in range(A_HEADS):
            o_h = out[h * rph:h * rph + t_new] - lam * out[h * rph + t_new:(h + 1) * rph]
            o_ref[:, h * LANES:(h + 1) * LANES] = _rms(o_h, sub_ref[...]) * out_scale


def _paged_diff_attention(q, k_new, v_new, cache_k, cache_v, page_table, layer_e, lam, subln, out_scale):
    b, t_new, _ = q.shape
    n_table = page_table.shape[1]
    assert n_table % DEC_PAGES == 0
    seq = lambda i, j, pt: (i, 0, 0)
    const = lambda i, j, pt: (0, 0)

    def page_spec(n):
        return pl.BlockSpec((None, None, PAGE_SIZE * A_HEADS, A_VD),
                            lambda i, j, pt: (layer_e, pt[i, j * DEC_PAGES + n], 0, 0))

    rows = A_HEADS * A_ROWS_PER_HEAD * t_new
    kern = functools.partial(_decode_kernel, n_pages=DEC_PAGES, t_new=t_new, out_scale=out_scale)
    grid_spec = pltpu.PrefetchScalarGridSpec(
        num_scalar_prefetch=1,
        grid=(b, n_table // DEC_PAGES),
        in_specs=[pl.BlockSpec((None, t_new, A_QKW), seq),
                  pl.BlockSpec((None, t_new, A_QKW), seq),
                  pl.BlockSpec((None, t_new, A_W), seq),
                  pl.BlockSpec((1, A_VD), const),
                  pl.BlockSpec((1, A_VD), const)]
                 + [page_spec(n) for n in range(DEC_PAGES)] * 2,
        out_specs=pl.BlockSpec((None, t_new, A_W), seq),
        scratch_shapes=[pltpu.VMEM((rows, A_QKW), BF16),
                        pltpu.VMEM((rows, 1), F32),
                        pltpu.VMEM((rows, 1), F32),
                        pltpu.VMEM((rows, A_VD), F32)],
    )
    lam_row = jnp.full((1, A_VD), lam, F32)
    return pl.pallas_call(
        kern,
        grid_spec=grid_spec,
        out_shape=jax.ShapeDtypeStruct((b, t_new, A_W), F32),
        compiler_params=pltpu.CompilerParams(dimension_semantics=("arbitrary", "arbitrary"),
                                             vmem_limit_bytes=VMEM_LIMIT),
        name="paged_diff_attention",
    )(page_table, q, k_new, v_new, subln.reshape(1, A_VD), lam_row,
      *([cache_k] * DEC_PAGES), *([cache_v] * DEC_PAGES))


ATT_TILE = 512


def _flash_kernel(q_ref, k_ref, v_ref, sub_ref, lam_ref, o_ref, qhat_s, m_s, l_s, acc_s, *, out_scale):
    qi = pl.program_id(2)
    ki = pl.program_id(3)
    tq = q_ref.shape[0]
    tk = k_ref.shape[0]

    @pl.when(ki == 0)
    def _():
        qt = q_ref[...] * (A_QK ** -0.5)
        lane = lax.broadcasted_iota(jnp.int32, (tq, LANES), 1)
        qhat_s[0:tq, :] = jnp.where(lane < A_QK, qt, 0.0).astype(BF16)
        qhat_s[tq:2 * tq, :] = jnp.where(lane < A_QK, 0.0, qt).astype(BF16)
        m_s[...] = jnp.full(m_s.shape, MASK_NEG, F32)
        l_s[...] = jnp.zeros(l_s.shape, F32)
        acc_s[...] = jnp.zeros(acc_s.shape, F32)

    def step(on_diagonal):
        s = lax.dot_general(qhat_s[...], k_ref[...].astype(BF16), NT_DIMS, preferred_element_type=F32)
        if on_diagonal:
            r = lax.broadcasted_iota(jnp.int32, (2 * tq, tk), 0)
            c = lax.broadcasted_iota(jnp.int32, (2 * tq, tk), 1)
            s = jnp.where(c <= jnp.where(r >= tq, r - tq, r), s, MASK_NEG)
        m_old = m_s[...]
        m_new = jnp.maximum(m_old, jnp.max(s, axis=-1, keepdims=True))
        alpha = jnp.exp(m_old - m_new)
        p = jnp.exp(s - m_new)
        l_s[...] = alpha * l_s[...] + jnp.sum(p, axis=-1, keepdims=True)
        acc_s[...] = alpha * acc_s[...] + jnp.dot(p.astype(BF16), v_ref[...].astype(BF16),
                                                  preferred_element_type=F32)
        m_s[...] = m_new

    @pl.when(ki < qi)
    def _():
        step(False)

    @pl.when(ki == qi)
    def _():
        step(True)
        out = acc_s[...] / l_s[...]
        o = out[0:tq] - lam_ref[...] * out[tq:2 * tq]
        o_ref[...] = _rms(o, sub_ref[...]) * out_scale


def _flash_diff_attention(q, k, v, n_batch, seq_len, lam, subln, out_scale):
    assert seq_len % ATT_TILE == 0
    nt = seq_len // ATT_TILE
    qmap = lambda b, h, qi, ki: (b * nt + qi, h)
    kmap = lambda b, h, qi, ki: (b * nt + jnp.minimum(ki, qi), h)
    const = lambda b, h, qi, ki: (0, 0)
    kern = functools.partial(_flash_kernel, out_scale=out_scale)
    return pl.pallas_call(
        kern,
        grid=(n_batch, A_HEADS, nt, nt),
        in_specs=[pl.BlockSpec((ATT_TILE, LANES), qmap),
                  pl.BlockSpec((ATT_TILE, LANES), kmap),
                  pl.BlockSpec((ATT_TILE, LANES), kmap),
                  pl.BlockSpec((1, A_VD), const),
                  pl.BlockSpec((1, A_VD), const)],
        out_specs=pl.BlockSpec((ATT_TILE, LANES), qmap),
        out_shape=jax.ShapeDtypeStruct((n_batch * seq_len, A_W), F32),
        scratch_shapes=[pltpu.VMEM((2 * ATT_TILE, LANES), BF16),
                        pltpu.VMEM((2 * ATT_TILE, 1), F32),
                        pltpu.VMEM((2 * ATT_TILE, 1), F32),
                        pltpu.VMEM((2 * ATT_TILE, A_VD), F32)],
        compiler_params=pltpu.CompilerParams(dimension_semantics=("arbitrary",) * 4, vmem_limit_bytes=VMEM_LIMIT),
        name="flash_diff_attention",
    )(q, k, v, subln.reshape(1, A_VD), jnp.full((1, A_VD), lam, F32))


SUBLANES = 8
HGRN_CHUNK = 32
HGRN_BLOCK = 512


def _split3_dot(tri, x):
    hi = x.astype(BF16)
    r1 = x - hi.astype(F32)
    mid = r1.astype(BF16)
    lo = (r1 - mid.astype(F32)).astype(BF16)
    return (jnp.dot(tri, hi, preferred_element_type=F32) + jnp.dot(tri, mid, preferred_element_type=F32)
            + jnp.dot(tri, lo, preferred_element_type=F32))


def _cumsum_rows(x, tri):
    n = x.shape[0]
    if n > SUBLANES:
        return _split3_dot(tri, x)
    row = lax.broadcasted_iota(jnp.int32, x.shape, 0)
    shift = 1
    while shift < n:
        x = x + jnp.where(row >= shift, pltpu.roll(x, shift, 0), 0.0)
        shift *= 2
    return x


def _hgrn2_kernel(*refs, chunk, has_s0, n_seq):
    if has_s0:
        q_ref, f_ref, i_ref, g_ref, lb_ref, nw_ref, s0_ref, o_ref, sout_ref, st_s = refs
    else:
        q_ref, f_ref, i_ref, g_ref, lb_ref, nw_ref, o_ref, sout_ref, st_s = refs
        s0_ref = None
    tb = pl.program_id(1)
    n_chunks = q_ref.shape[0] // chunk
    n_tiles = chunk // SUBLANES

    def load_state(i):
        for h in range(B_HEADS):
            st_s[h] = s0_ref[i, h].T if has_s0 else jnp.zeros((B_DV, B_DK), F32)

    def store_state(i):
        for h in range(B_HEADS):
            sout_ref[i, h] = st_s[h].T

    nw = nw_ref[...]
    tri = (lax.broadcasted_iota(jnp.int32, (chunk, chunk), 0)
           >= lax.broadcasted_iota(jnp.int32, (chunk, chunk), 1)).astype(BF16)
    row8 = lax.broadcasted_iota(jnp.int32, (SUBLANES, 1), 0)
    mxu = (lambda a: a.astype(BF16)) if chunk >= 2 * SUBLANES else (lambda a: a)

    def head_chunk(r0, h):
        cols = slice(h * B_DK, (h + 1) * B_DK)
        lb = lb_ref[:, cols]
        log_lb = jnp.log(jnp.maximum(lb, LB_FLOOR))
        z = f_ref[pl.ds(r0, chunk), cols]
        qq = q_ref[pl.ds(r0, chunk), cols]
        vv = i_ref[pl.ds(r0, chunk), cols]
        gg = g_ref[pl.ds(r0, chunk), cols]
        log_sig = jnp.minimum(z, 0.0) - jnp.log1p(jnp.exp(-jnp.abs(z)))
        arg = jnp.log1p(-lb) + log_sig
        log_f = jnp.maximum(log_lb, arg) + jnp.log1p(jnp.exp(-jnp.abs(log_lb - arg)))
        kk = (1.0 - lb) * jax.nn.sigmoid(-z)
        qf = qq * jax.nn.sigmoid(qq)
        gcum = _cumsum_rows(log_f, tri)
        g_last = gcum[chunk - 1:chunk, :]
        st = st_s[h]
        o_inter = lax.dot_general(mxu(qf * jnp.exp(gcum)), mxu(st), NT_DIMS, preferred_element_type=F32)
        g_t = [gcum[i * SUBLANES:(i + 1) * SUBLANES] for i in range(n_tiles)]
        q_t = [qf[i * SUBLANES:(i + 1) * SUBLANES] for i in range(n_tiles)]
        o_t = [None] * n_tiles
        for s in range(chunk):
            g_s = gcum[s:s + 1, :]
            k_s = kk[s:s + 1, :]
            v_s = vv[s:s + 1, :]
            for ti in range(s // SUBLANES, n_tiles):
                e = jnp.exp(jnp.minimum(g_t[ti] - g_s, 0.0))
                col = jnp.sum(q_t[ti] * (k_s * e), axis=-1, keepdims=True)
                if ti == s // SUBLANES:
                    col = jnp.where(row8 >= s % SUBLANES, col, 0.0)
                term = col * v_s
                o_t[ti] = term if o_t[ti] is None else o_t[ti] + term
        ob = o_inter + jnp.concatenate(o_t, axis=0)
        kw = kk * jnp.exp(g_last - gcum)
        st_s[h] = st * jnp.exp(g_last) + lax.dot_general(mxu(vv), mxu(kw), TN_DIMS, preferred_element_type=F32)
        o_ref[pl.ds(r0, chunk), cols] = _rms(ob, nw) * jax.nn.sigmoid(gg)

    def body(c, carry):
        r0 = c * chunk if isinstance(c, int) else pl.multiple_of(c * chunk, chunk)
        for h in range(B_HEADS):
            head_chunk(r0, h)
        return carry

    if n_seq == 1:
        @pl.when(tb == 0)
        def _():
            load_state(0)

        lax.fori_loop(0, n_chunks, body, 0)

        @pl.when(tb == pl.num_programs(1) - 1)
        def _():
            store_state(0)
    else:
        for i in range(n_seq):
            load_state(i)
            body(i, 0)
            store_state(i)


HGRN_SEQS_PER_STEP = 8


def _hgrn2(bq, bf, bi, bg, lb, b_norm_w, n_batch, seq_len, row_off, s0, layer_e):
    if seq_len <= HGRN_CHUNK:
        n_seq = math.gcd(n_batch, HGRN_SEQS_PER_STEP)
        chunk = seq_len
        blk = n_seq * seq_len
        nb = 1
    else:
        n_seq = 1
        blk = min(HGRN_BLOCK, seq_len)
        chunk = math.gcd(blk, HGRN_CHUNK)
        nb = seq_len // blk
    assert seq_len % chunk == 0 and chunk % SUBLANES == 0 and row_off % blk == 0
    off = row_off // blk
    rmap = lambda b, t: (off + b * nb + t, 0)
    omap = lambda b, t: (b * nb + t, 0)
    const = lambda b, t: (0, 0)
    in_specs = [pl.BlockSpec((blk, B_KW), rmap)] * 4 + [pl.BlockSpec((1, B_KW), const), pl.BlockSpec((1, B_DV), const)]
    args = [bq, bf, bi, bg, lb.reshape(1, B_KW), b_norm_w.reshape(1, B_DV)]
    if s0 is not None:
        in_specs.append(pl.BlockSpec((None, n_seq, B_HEADS, B_DK, B_DV), lambda b, t: (layer_e, b, 0, 0, 0)))
        args.append(s0)
    kern = functools.partial(_hgrn2_kernel, chunk=chunk, has_s0=s0 is not None, n_seq=n_seq)
    return pl.pallas_call(
        kern,
        grid=(n_batch // n_seq, nb),
        in_specs=in_specs,
        out_specs=[pl.BlockSpec((blk, B_W), omap),
                   pl.BlockSpec((n_seq, B_HEADS, B_DK, B_DV), lambda b, t: (b, 0, 0, 0))],
        out_shape=[jax.ShapeDtypeStruct((n_batch * seq_len, B_W), F32),
                   jax.ShapeDtypeStruct((n_batch, B_HEADS, B_DK, B_DV), F32)],
        scratch_shapes=[pltpu.VMEM((B_HEADS, B_DV, B_DK), F32)],
        compiler_params=pltpu.CompilerParams(dimension_semantics=("arbitrary",) * 2, vmem_limit_bytes=VMEM_LIMIT),
        name="hgrn2",
    )(*args)


SSD_CHUNK = 128
C_PAIRS = C_HEADS // 2
C_GROUP_W = C_DINNER // C_GROUPS


def _softplus(x):
    return jnp.maximum(x, 0.0) + jnp.log1p(jnp.exp(-jnp.abs(x)))


def _ssd_kernel(xbc_ref, z_ref, dt_ref, cw_ref, cb_ref, dtb_ref, alog_ref, cd_ref, nw_ref,
                y_ref, sout_ref, cout_ref, xpad_s, st_s):
    c = pl.program_id(1)
    L = xbc_ref.shape[0]
    tail = C_CONV - 1

    @pl.when(c == 0)
    def _():
        xpad_s[0:SUBLANES, :] = jnp.zeros((SUBLANES, C_CONV_DIM), F32)
        st_s[...] = jnp.zeros(st_s.shape, F32)

    xpad_s[SUBLANES:SUBLANES + L, :] = xbc_ref[...]
    acc = cb_ref[...] + cw_ref[0:1, :] * xpad_s[pl.ds(SUBLANES - tail, L), :]
    for j in range(1, C_CONV):
        acc = acc + cw_ref[j:j + 1, :] * xpad_s[pl.ds(SUBLANES - tail + j, L), :]
    xc = acc * jax.nn.sigmoid(acc)

    @pl.when(c == pl.num_programs(1) - 1)
    def _():
        cout_ref[...] = xpad_s[pl.ds(SUBLANES + L - tail, tail), :]

    xpad_s[0:SUBLANES, :] = xpad_s[L:L + SUBLANES, :]

    dtv = _softplus(dt_ref[...] + dtb_ref[...])
    a = dtv * (-jnp.exp(alog_ref[...]))
    row = lax.broadcasted_iota(jnp.int32, (L, L), 0)
    col = lax.broadcasted_iota(jnp.int32, (L, L), 1)
    causal = row >= col
    gcum = _split3_dot(causal.astype(BF16), a)
    gcum_t = gcum.T
    lane = lax.broadcasted_iota(jnp.int32, (L, LANES), 1)
    low = lane < C_HEADDIM
    srow = lax.broadcasted_iota(jnp.int32, (2 * C_HEADDIM, C_DSTATE), 0)

    for g in range(C_GROUPS):
        b_g = xc[:, C_DINNER + g * C_DSTATE:C_DINNER + (g + 1) * C_DSTATE].astype(BF16)
        c_g = xc[:, C_DINNER + (C_GROUPS + g) * C_DSTATE:C_DINNER + (C_GROUPS + g + 1) * C_DSTATE].astype(BF16)
        cb = lax.dot_general(c_g, b_g, NT_DIMS, preferred_element_type=F32)
        u_parts = []
        sq = None
        for jp in range(C_PAIRS // C_GROUPS):
            pair = g * (C_PAIRS // C_GROUPS) + jp
            x_pair = xc[:, pair * LANES:(pair + 1) * LANES]
            y_pair = None
            gb = []
            dtb = []
            for hh in range(2):
                h = 2 * pair + hh
                g_col = jnp.broadcast_to(gcum[:, h:h + 1], (L, LANES))
                dt_col = jnp.broadcast_to(dtv[:, h:h + 1], (L, LANES))
                gb.append(g_col)
                dtb.append(dt_col)
                dec = jnp.exp(jnp.minimum(g_col - gcum_t[h:h + 1, :], 0.0))
                m_h = jnp.where(causal, cb * dec, 0.0).astype(BF16)
                half = low if hh == 0 else jnp.logical_not(low)
                xdt = jnp.where(half, x_pair * dt_col, 0.0).astype(BF16)
                d = jnp.dot(m_h, xdt, preferred_element_type=F32)
                y_pair = d if y_pair is None else y_pair + d
            st = st_s[pair]
            y_inter = lax.dot_general(c_g, st.astype(BF16), NT_DIMS, preferred_element_type=F32)
            y_pair = y_pair + y_inter * jnp.where(low, jnp.exp(gb[0]), jnp.exp(gb[1]))
            y_pair = y_pair + cd_ref[:, pair * LANES:(pair + 1) * LANES] * x_pair
            gl = [gb[0][L - 1:L, :], gb[1][L - 1:L, :]]
            w_pair = jnp.where(low, jnp.exp(gl[0] - gb[0]) * dtb[0], jnp.exp(gl[1] - gb[1]) * dtb[1])
            decay = jnp.where(srow < C_HEADDIM, jnp.exp(gl[0]), jnp.exp(gl[1]))
            st_s[pair] = decay * st + lax.dot_general((x_pair * w_pair).astype(BF16), b_g, TN_DIMS,
                                                      preferred_element_type=F32)
            zz = z_ref[:, pair * LANES:(pair + 1) * LANES]
            u = y_pair * (zz * jax.nn.sigmoid(zz))
            u_parts.append(u)
            s2 = jnp.sum(u * u, axis=-1, keepdims=True)
            sq = s2 if sq is None else sq + s2
        inv = lax.rsqrt(sq * (1.0 / C_GROUP_W) + EPS)
        for jp, u in enumerate(u_parts):
            pair = g * (C_PAIRS // C_GROUPS) + jp
            y_ref[:, pair * LANES:(pair + 1) * LANES] = u * inv * nw_ref[:, pair * LANES:(pair + 1) * LANES]

    @pl.when(c == pl.num_programs(1) - 1)
    def _():
        sout_ref[...] = st_s[...]


def _ssd_prompt(xbc, z, dt, conv_w, conv_b, dt_bias, a_log, c_d, c_norm_w, n_batch, seq_len):
    assert seq_len % SSD_CHUNK == 0
    nc = seq_len // SSD_CHUNK
    rmap = lambda b, c: (b * nc + c, 0)
    const = lambda b, c: (0, 0)
    pad16 = lambda v: jnp.pad(v.astype(F32), (0, DT_PAD - C_HEADS)).reshape(1, DT_PAD)
    y, s_out, c_out = pl.pallas_call(
        _ssd_kernel,
        grid=(n_batch, nc),
        in_specs=[pl.BlockSpec((SSD_CHUNK, C_CONV_DIM), rmap),
                  pl.BlockSpec((SSD_CHUNK, C_DINNER), rmap),
                  pl.BlockSpec((SSD_CHUNK, DT_PAD), rmap),
                  pl.BlockSpec((C_CONV, C_CONV_DIM), const),
                  pl.BlockSpec((1, C_CONV_DIM), const),
                  pl.BlockSpec((1, DT_PAD), const),
                  pl.BlockSpec((1, DT_PAD), const),
                  pl.BlockSpec((1, C_DINNER), const),
                  pl.BlockSpec((1, C_DINNER), const)],
        out_specs=[pl.BlockSpec((SSD_CHUNK, C_DINNER), rmap),
                   pl.BlockSpec((None, C_PAIRS, 2 * C_HEADDIM, C_DSTATE), lambda b, c: (b, 0, 0, 0)),
                   pl.BlockSpec((None, C_CONV - 1, C_CONV_DIM), lambda b, c: (b, 0, 0))],
        out_shape=[jax.ShapeDtypeStruct((n_batch * seq_len, C_DINNER), F32),
                   jax.ShapeDtypeStruct((n_batch, C_PAIRS, 2 * C_HEADDIM, C_DSTATE), F32),
                   jax.ShapeDtypeStruct((n_batch, C_CONV - 1, C_CONV_DIM), F32)],
        scratch_shapes=[pltpu.VMEM((SUBLANES + SSD_CHUNK, C_CONV_DIM), F32),
                        pltpu.VMEM((C_PAIRS, 2 * C_HEADDIM, C_DSTATE), F32)],
        compiler_params=pltpu.CompilerParams(dimension_semantics=("arbitrary", "arbitrary"),
                                             vmem_limit_bytes=VMEM_LIMIT),
        name="ssd_prompt",
    )(xbc, z, dt, conv_w, conv_b.reshape(1, C_CONV_DIM), pad16(dt_bias), pad16(a_log),
      jnp.repeat(c_d.astype(F32), C_HEADDIM).reshape(1, C_DINNER), c_norm_w.reshape(1, C_DINNER))
    return y, s_out.reshape(n_batch, C_HEADS, C_HEADDIM, C_DSTATE), c_out


def _swa_kernel(q_ref, kp_ref, kc_ref, vp_ref, vc_ref, sink_ref, o_ref):
    i = pl.program_id(1)
    w = q_ref.shape[0]
    kk = jnp.concatenate([kp_ref[...], kc_ref[...]], axis=0).astype(BF16)
    vv = jnp.concatenate([vp_ref[...], vc_ref[...]], axis=0).astype(BF16)
    rows = D_GROUP * w
    r = lax.broadcasted_iota(jnp.int32, (rows, 2 * w), 0) % w
    c = lax.broadcasted_iota(jnp.int32, (rows, 2 * w), 1)
    first_key = jnp.where(i > 0, 0, w)
    visible = (c <= w + r) & (c > r) & (c >= first_key)
    lane = lax.broadcasted_iota(jnp.int32, (w, LANES), 1)
    low = lane < D_HD
    for g in range(D_KV_HEADS):
        g_half = low if g == 0 else jnp.logical_not(low)
        q_parts = []
        sink_parts = []
        for rr in range(D_GROUP):
            h = g * D_GROUP + rr
            pair, hh = divmod(h, 2)
            q_pair = q_ref[:, pair * LANES:(pair + 1) * LANES] * (D_HD ** -0.5)
            q_src = q_pair if hh == g else pltpu.roll(q_pair, D_HD, 1)
            q_parts.append(jnp.where(g_half, q_src, 0.0).astype(BF16))
            sink_parts.append(jnp.broadcast_to(sink_ref[h:h + 1, 0:1], (w, 1)))
        qg = jnp.concatenate(q_parts, axis=0)
        sink = jnp.concatenate(sink_parts, axis=0)
        s = lax.dot_general(qg, kk, NT_DIMS, preferred_element_type=F32)
        s = jnp.where(visible, s, MASK_NEG)
        m = jnp.maximum(jnp.max(s, axis=-1, keepdims=True), sink)
        p = jnp.exp(s - m)
        p = p / (jnp.sum(p, axis=-1, keepdims=True) + jnp.exp(sink - m))
        o = jnp.dot(p.astype(BF16), vv, preferred_element_type=F32)
        for jp in range(D_GROUP // 2):
            pair = g * (D_GROUP // 2) + jp
            halves = []
            for hh in range(2):
                o_h = o[(2 * jp + hh) * w:(2 * jp + hh + 1) * w]
                halves.append(o_h if hh == g else pltpu.roll(o_h, D_HD, 1))
            o_ref[:, pair * LANES:(pair + 1) * LANES] = jnp.where(low, halves[0], halves[1])


def _swa_prompt_attention(dq, dk, dv, sinks, n_batch, seq_len):
    assert seq_len % WINDOW == 0
    nb = seq_len // WINDOW
    cur = lambda b, i: (b * nb + i, 0)
    prev = lambda b, i: (b * nb + jnp.maximum(i - 1, 0), 0)
    kv_w = D_KV_HEADS * D_HD
    return pl.pallas_call(
        _swa_kernel,
        grid=(n_batch, nb),
        in_specs=[pl.BlockSpec((WINDOW, D_HEADS * D_HD), cur),
                  pl.BlockSpec((WINDOW, kv_w), prev),
                  pl.BlockSpec((WINDOW, kv_w), cur),
                  pl.BlockSpec((WINDOW, kv_w), prev),
                  pl.BlockSpec((WINDOW, kv_w), cur),
                  pl.BlockSpec((D_HEADS, LANES), lambda b, i: (0, 0))],
        out_specs=pl.BlockSpec((WINDOW, D_HEADS * D_HD), cur),
        out_shape=jax.ShapeDtypeStruct((n_batch * seq_len, D_HEADS * D_HD), F32),
        compiler_params=pltpu.CompilerParams(dimension_semantics=("arbitrary", "arbitrary"),
                                             vmem_limit_bytes=VMEM_LIMIT),
        name="swa_prompt",
    )(dq, dk, dk, dv, dv, jnp.broadcast_to(sinks.astype(F32)[:, None], (D_HEADS, LANES)))


def _rmsnorm_j(x, g):
    xf = x.astype(F32)
    r = xf * lax.rsqrt(jnp.mean(xf * xf, axis=-1, keepdims=True) + EPS)
    return (r * g.astype(F32)).astype(x.dtype)


def _to_chunks(a, L):
    b, t = a.shape[:2]
    return jnp.moveaxis(a.reshape((b, t // L, L) + a.shape[2:]), 1, 0)


def _from_chunks(a):
    n, b, L = a.shape[:3]
    return jnp.moveaxis(a, 0, 1).reshape((b, n * L) + a.shape[3:])


def _diff_lambda(lam, layer):
    lam = lam.astype(F32)
    lam_init = 0.8 - 0.6 * math.exp(-0.3 * layer)
    lam_full = jnp.exp(jnp.sum(lam[0] * lam[1])) - jnp.exp(jnp.sum(lam[2] * lam[3])) + lam_init
    return lam_full, lam_init


def _diff_attend(q, parts, lam):
    scale = A_QK ** -0.5
    scores = []
    for k, _, m in parts:
        s = jnp.einsum("bqhcd,bkhcd->bhcqk", q, k).astype(F32) * scale
        scores.append(s if m is None else jnp.where(m, s, MASK_NEG))
    p = jax.nn.softmax(jnp.concatenate(scores, axis=-1), axis=-1)
    p = p[:, :, 0] - lam * p[:, :, 1]
    out = None
    off = 0
    for k, v, _ in parts:
        n = k.shape[1]
        o = jnp.einsum("bhqk,bkhd->bqhd", p[..., off:off + n].astype(v.dtype), v)
        out = o if out is None else out + o
        off += n
    return out


def _diff_attn_prompt(q, k, v, lam):
    b, t = q.shape[:2]
    nb = t // A_Q_BLOCK
    qb = jnp.moveaxis(q.reshape(b, nb, A_Q_BLOCK, A_HEADS, 2, A_QK), 1, 0)
    k_pos = jnp.arange(t)

    def block(args):
        qi, i = args
        q_pos = i * A_Q_BLOCK + jnp.arange(A_Q_BLOCK)
        mask = k_pos[None, :] <= q_pos[:, None]
        return _diff_attend(qi, [(k, v, mask)], lam)

    o = lax.map(block, (qb, jnp.arange(nb)))
    return jnp.moveaxis(o, 0, 1).reshape(b, t, A_HEADS, A_VD)


def _diff_attn_sample(q, k, v, k_past, v_past, lam):
    t = q.shape[1]
    causal = jnp.tril(jnp.ones((t, t), bool))
    return _diff_attend(q, [(k_past, v_past, None), (k, v, causal)], lam)


def _hgrn_lower_bound(p):
    sm = jax.nn.softmax(p.astype(F32), axis=0)
    return jnp.concatenate([jnp.zeros_like(sm[:1]), jnp.cumsum(sm[1:], axis=0)], axis=0)


def _hgrn2_scan(q, f_logit, i, lb, S0):
    t = q.shape[1]
    L = math.gcd(t, B_CHUNK)
    z = f_logit.astype(F32)
    log_lb = jnp.log(jnp.maximum(lb, LB_FLOOR))
    log_f = jnp.logaddexp(log_lb, jnp.log1p(-lb) + jax.nn.log_sigmoid(z))
    k = (1.0 - lb) * jax.nn.sigmoid(-z)
    qf = jax.nn.silu(q.astype(F32))
    vf = i.astype(F32)
    causal = jnp.tril(jnp.ones((L, L), bool))

    def step(S, c):
        qc, kc, vc, gc = c
        G = jnp.cumsum(gc, axis=1)
        o_inter = jnp.einsum("blhk,bhkv->blhv", qc * jnp.exp(G), S)
        rel = jnp.where(causal[None, :, :, None, None], G[:, :, None] - G[:, None], MASK_NEG)
        A = jnp.einsum("bthk,bshk,btshk->bhts", qc, kc, jnp.exp(rel))
        o = o_inter + jnp.einsum("bhts,bshv->bthv", A, vc)
        GL = G[:, -1]
        S = jnp.exp(GL)[..., None] * S + jnp.einsum("bshk,bshv->bhkv", kc * jnp.exp(GL[:, None] - G), vc)
        return S, o

    S, o = lax.scan(step, S0.astype(F32), (_to_chunks(qf, L), _to_chunks(k, L), _to_chunks(vf, L), _to_chunks(log_f, L)))
    return _from_chunks(o), S


def _causal_conv(xbc, conv0, w, b):
    xp = jnp.concatenate([conv0, xbc], axis=1)
    y = lax.conv_general_dilated(xp, w.astype(xp.dtype)[:, None, :], window_strides=(1,), padding="VALID",
                                 dimension_numbers=("NWC", "WIO", "NWC"), feature_group_count=xp.shape[-1])
    return jax.nn.silu(y + b.astype(xp.dtype)), xp[:, -(C_CONV - 1):]


def _ssd_scan(x, dt, Bm, Cm, A, S0):
    t = x.shape[1]
    L = math.gcd(t, C_CHUNK)
    a = dt * A
    causal = jnp.tril(jnp.ones((L, L), bool))

    def step(S, c):
        xc, dtc, ac, Bc, Cc = c
        G = jnp.cumsum(ac, axis=1)
        rel = jnp.where(causal[None, :, :, None], G[:, :, None] - G[:, None], MASK_NEG)
        M = jnp.einsum("bthn,bshn->btsh", Cc, Bc) * jnp.exp(rel) * dtc[:, None]
        y = jnp.einsum("btsh,bshp->bthp", M, xc) + jnp.einsum("bthn,bhpn->bthp", Cc, S) * jnp.exp(G)[..., None]
        GL = G[:, -1]
        S = jnp.exp(GL)[:, :, None, None] * S + jnp.einsum("bsh,bshp,bshn->bhpn", jnp.exp(GL[:, None] - G) * dtc, xc, Bc)
        return S, y

    S, y = lax.scan(step, S0.astype(F32),
                    (_to_chunks(x, L), _to_chunks(dt, L), _to_chunks(a, L), _to_chunks(Bm, L), _to_chunks(Cm, L)))
    return _from_chunks(y), S


def _gated_rmsnorm(y, z, g):
    b, t, _ = y.shape
    u = (y * jax.nn.silu(z.astype(F32))).reshape(b, t, C_GROUPS, C_DINNER // C_GROUPS)
    u = u * lax.rsqrt(jnp.mean(u * u, axis=-1, keepdims=True) + EPS)
    return u.reshape(b, t, C_DINNER) * g.astype(F32)


def _window_attend(q, k, v, mask, sinks):
    b, tq = q.shape[:2]
    qg = q.reshape(b, tq, D_KV_HEADS, D_GROUP, D_HD)
    s = jnp.einsum("bqgrd,bkgd->bgrqk", qg, k).astype(F32) * (D_HD ** -0.5)
    s = jnp.where(mask, s, MASK_NEG)
    sink = sinks.astype(F32).reshape(1, D_KV_HEADS, D_GROUP, 1, 1)
    m = jnp.maximum(jnp.max(s, axis=-1, keepdims=True), sink)
    p = jnp.exp(s - m)
    p = p / (jnp.sum(p, axis=-1, keepdims=True) + jnp.exp(sink - m))
    o = jnp.einsum("bgrqk,bkgd->bqgrd", p.astype(v.dtype), v)
    return o.reshape(b, tq, D_HEADS, D_HD)


def _swa_prompt(q, k, v, sinks):
    b, t = q.shape[:2]
    nb = t // WINDOW
    kb = k.reshape(b, nb, WINDOW, D_KV_HEADS, D_HD)
    vb = v.reshape(b, nb, WINDOW, D_KV_HEADS, D_HD)
    kk = jnp.concatenate([jnp.concatenate([jnp.zeros_like(kb[:, :1]), kb[:, :-1]], axis=1), kb], axis=2)
    vv = jnp.concatenate([jnp.concatenate([jnp.zeros_like(vb[:, :1]), vb[:, :-1]], axis=1), vb], axis=2)
    q_rel = WINDOW + jnp.arange(WINDOW)
    k_rel = jnp.arange(2 * WINDOW)
    band = (k_rel[None, :] <= q_rel[:, None]) & (k_rel[None, :] > q_rel[:, None] - WINDOW)
    valid = (jnp.arange(nb)[:, None] > 0) | (k_rel[None, :] >= WINDOW)
    mask = band[None] & valid[:, None, :]
    o = _window_attend(q.reshape(b * nb, WINDOW, D_HEADS, D_HD), kk.reshape(b * nb, 2 * WINDOW, D_KV_HEADS, D_HD),
                       vv.reshape(b * nb, 2 * WINDOW, D_KV_HEADS, D_HD), jnp.tile(mask, (b, 1, 1))[:, None, None], sinks)
    return o.reshape(b, t, D_HEADS, D_HD)


def _swa_sample(q, k, v, kc, vc, sinks):
    t = q.shape[1]
    w0 = kc.shape[1]
    kk = jnp.concatenate([kc.astype(k.dtype), k], axis=1)
    vv = jnp.concatenate([vc.astype(v.dtype), v], axis=1)
    q_rel = w0 + jnp.arange(t)
    k_rel = jnp.arange(w0 + t)
    mask = (k_rel[None, :] <= q_rel[:, None]) & (k_rel[None, :] > q_rel[:, None] - WINDOW)
    o = _window_attend(q, kk, vv, mask, sinks)
    return o, kk[:, -WINDOW:], vv[:, -WINDOW:]


def _even_mixer(parts, b, t, layer, e, w, past):
    aq, ak, av, bq, bf, bi, bg = parts
    q = aq.reshape(b, t, A_HEADS, 2, A_QK)
    k = ak.reshape(b, t, A_HEADS, 2, A_QK)
    v = av.reshape(b, t, A_HEADS, A_VD)
    lam, lam_init = _diff_lambda(w["a_lambda"][e], layer)
    if past is None:
        o = _diff_attn_prompt(q, k, v, lam)
        S0 = jnp.zeros((b, B_HEADS, B_DK, B_DV), F32)
    else:
        pt = past["page_table"]
        p_len = pt.shape[1] * PAGE_SIZE
        k_past = past["a_k"][e, pt].reshape(b, p_len, A_HEADS, 2, A_QK)
        v_past = past["a_v"][e, pt].reshape(b, p_len, A_HEADS, A_VD)
        o = _diff_attn_sample(q, k, v, k_past, v_past, lam)
        S0 = past["b_state"][e]
    a_out = _rmsnorm_j(o, w["a_subln"][e]) * (1.0 - lam_init)
    lb = _hgrn_lower_bound(w["b_lower_bounds"])[e].reshape(B_HEADS, B_DK)
    ob, S_new = _hgrn2_scan(bq.reshape(b, t, B_HEADS, B_DK), bf.reshape(b, t, B_HEADS, B_DK),
                            bi.reshape(b, t, B_HEADS, B_DV), lb, S0)
    b_out = _rmsnorm_j(ob, w["b_norm"][e]) * jax.nn.sigmoid(bg.reshape(b, t, B_HEADS, B_DV))
    mix = jnp.concatenate([a_out.reshape(b * t, A_W), b_out.reshape(b * t, B_W)], axis=-1)
    return mix, (k.reshape(b, t, A_HEADS, 2 * A_QK), v, S_new)


def _odd_mixer(parts, b, t, e, w, past):
    z, xbc, dt, dq, dk, dv = parts
    z = z.reshape(b, t, C_DINNER)
    xbc = xbc.reshape(b, t, C_CONV_DIM)
    dt = dt[:, :C_HEADS].reshape(b, t, C_HEADS)
    if past is None:
        conv0 = jnp.zeros((b, C_CONV - 1, C_CONV_DIM), xbc.dtype)
        S0 = jnp.zeros((b, C_HEADS, C_HEADDIM, C_DSTATE), F32)
    else:
        conv0 = past["c_conv"][e]
        S0 = past["c_ssm"][e]
    xbc, conv_new = _causal_conv(xbc, conv0, w["c_conv_w"][e], w["c_conv_b"][e])
    xs, Bm, Cm = jnp.split(xbc, [C_DINNER, C_DINNER + C_GROUPS * C_DSTATE], axis=-1)
    rep = C_HEADS // C_GROUPS
    xs = xs.reshape(b, t, C_HEADS, C_HEADDIM)
    Bm = jnp.repeat(Bm.reshape(b, t, C_GROUPS, C_DSTATE), rep, axis=2)
    Cm = jnp.repeat(Cm.reshape(b, t, C_GROUPS, C_DSTATE), rep, axis=2)
    dt = jax.nn.softplus(dt + w["c_dt_bias"][e])
    A = -jnp.exp(w["c_a_log"][e])
    y, S_new = _ssd_scan(xs, dt, Bm, Cm, A, S0)
    y = y + w["c_d"][e][:, None] * xs
    y = _gated_rmsnorm(y.reshape(b, t, C_DINNER), z, w["c_norm"][e])
    q = dq.reshape(b, t, D_HEADS, D_HD)
    k = dk.reshape(b, t, D_KV_HEADS, D_HD)
    v = dv.reshape(b, t, D_KV_HEADS, D_HD)
    if past is None:
        o = _swa_prompt(q, k, v, w["d_sinks"][e])
        k_win, v_win = k[:, -WINDOW:], v[:, -WINDOW:]
    else:
        o, k_win, v_win = _swa_sample(q, k, v, past["d_k"][e], past["d_v"][e], w["d_sinks"][e])
    mix = jnp.concatenate([y.reshape(b * t, C_DINNER), o.reshape(b * t, D_HEADS * D_HD)], axis=-1)
    return mix, (S_new, conv_new, k_win, v_win)


def _rope_tables(pos):
    inv = ROPE_THETA ** (-jnp.arange(0, A_QK, 2, dtype=F32) / A_QK)
    ang = pos.astype(F32)[:, None] * inv[None, :]
    cos = jnp.tile(jnp.cos(ang), (1, LANES // ROPE_HALF))
    sin = jnp.tile(jnp.sin(ang), (1, LANES // ROPE_HALF))
    first_half = (jnp.arange(LANES) % (2 * ROPE_HALF)) < ROPE_HALF
    sin_lo = jnp.where(first_half[None, :], -sin, 0.0)
    sin_hi = jnp.where(first_half[None, :], 0.0, sin)
    return cos, sin_lo, sin_hi


def kernel(x_prompt, x_sample, cache_a_k, cache_a_v, state_b, state_c_ssm, state_c_conv, cache_d_k, cache_d_v,
           page_table, norm_w, mlp_w1, mlp_w2, w_in_e, w_out_e, a_lambda, a_subln, b_lower_bounds, b_norm,
           w_in_o, w_out_o, c_conv_w, c_conv_b, c_dt_bias, c_a_log, c_d, c_norm, d_sinks):
    w = {"a_lambda": a_lambda, "a_subln": a_subln, "b_lower_bounds": b_lower_bounds, "b_norm": b_norm,
         "c_conv_w": c_conv_w, "c_conv_b": c_conv_b, "c_dt_bias": c_dt_bias, "c_a_log": c_a_log,
         "c_d": c_d, "c_norm": c_norm, "d_sinks": d_sinks}
    bp, tp, _ = x_prompt.shape
    bs, ts, _ = x_sample.shape
    mp, ms = bp * tp, bs * ts
    past_len = page_table.shape[1] * PAGE_SIZE
    pos = jnp.concatenate([jnp.tile(jnp.arange(tp), bp), jnp.tile(past_len + jnp.arange(ts), bs)])
    rope_tabs = _rope_tables(pos)
    past = {"page_table": page_table, "a_k": cache_a_k, "a_v": cache_a_v, "b_state": state_b,
            "c_ssm": state_c_ssm, "c_conv": state_c_conv, "d_k": cache_d_k, "d_v": cache_d_v}

    w_in_e_b = w_in_e.astype(BF16)
    dt_off = C_DINNER + C_CONV_DIM
    w_in_o_b = jnp.concatenate(
        [w_in_o[:, :, :dt_off],
         jnp.pad(w_in_o[:, :, dt_off:dt_off + C_HEADS], ((0, 0), (0, 0), (0, DT_PAD - C_HEADS))),
         w_in_o[:, :, dt_off + C_HEADS:]], axis=-1).astype(BF16)
    n_pool = cache_a_k.shape[1]
    cache_k4 = cache_a_k.reshape(cache_a_k.shape[0], n_pool, PAGE_SIZE * A_HEADS, 2 * A_QK)
    cache_v4 = cache_a_v.reshape(cache_a_v.shape[0], n_pool, PAGE_SIZE * A_HEADS, A_VD)
    w_out_e_b = w_out_e.astype(BF16)
    w_out_o_b = w_out_o.astype(BF16)
    w1_b = mlp_w1.astype(BF16)
    w2_b = mlp_w2.astype(BF16)

    even_widths = (A_QKW, A_QKW, A_W, B_KW, B_KW, B_W, B_W)
    even_rope = (True, True, False, False, False, False, False)
    odd_widths = (C_DINNER, C_CONV_DIM, DT_PAD, D_HEADS * D_HD, D_KV_HEADS * D_HD, D_KV_HEADS * D_HD)
    odd_rope = (False, False, False, True, True, False)

    x = jnp.concatenate([x_prompt.reshape(mp, D_MODEL), x_sample.reshape(ms, D_MODEL)], axis=0)
    st_p = {n: [] for n in ("a_k", "a_v", "b_state", "c_ssm", "c_conv", "d_k", "d_v")}
    st_s = {n: [] for n in st_p}
    for l in range(DEPTH):
        g = norm_w[l]
        e = l // 2
        if l % 2 == 0:
            aq, ak, av, bq, bf, bi, bg = _in_projection(x, g[0:1], w_in_e_b[e], rope_tabs, even_widths, even_rope)
            lam, lam_init = _diff_lambda(a_lambda[e], l)
            out_scale = 1.0 - lam_init
            a_p = _flash_diff_attention(aq, ak, av, bp, tp, lam, a_subln[e], out_scale)
            shp = (bs, ts, A_QKW)
            a_s = _paged_diff_attention(aq[mp:].reshape(shp), ak[mp:].reshape(shp), av[mp:].reshape(shp),
                                        cache_k4, cache_v4, page_table, e, lam, a_subln[e], out_scale)
            lb = _hgrn_lower_bound(b_lower_bounds)[e]
            b_p, bst_p = _hgrn2(bq, bf, bi, bg, lb, b_norm[e], bp, tp, 0, None, e)
            b_s, bst_s = _hgrn2(bq, bf, bi, bg, lb, b_norm[e], bs, ts, mp, state_b, e)
            mix_groups = [[a_p, b_p], [a_s.reshape(ms, A_W), b_s]]
            st_p["a_k"].append(ak[:mp].reshape(bp, tp, A_HEADS, 2 * A_QK))
            st_p["a_v"].append(av[:mp].reshape(bp, tp, A_HEADS, A_VD))
            st_p["b_state"].append(bst_p)
            st_s["a_k"].append(ak[mp:].reshape(bs, ts, A_HEADS, 2 * A_QK))
            st_s["a_v"].append(av[mp:].reshape(bs, ts, A_HEADS, A_VD))
            st_s["b_state"].append(bst_s)
            w_out = w_out_e_b[e]
        else:
            parts = _in_projection(x, g[0:1], w_in_o_b[e], rope_tabs, odd_widths, odd_rope)
            z, xbc, dt, dq, dk, dv = parts
            y_p, ssm_p, conv_p = _ssd_prompt(xbc, z, dt, c_conv_w[e], c_conv_b[e], c_dt_bias[e], c_a_log[e],
                                             c_d[e], c_norm[e], bp, tp)
            k_p = dk[:mp].reshape(bp, tp, D_KV_HEADS, D_HD)
            v_p = dv[:mp].reshape(bp, tp, D_KV_HEADS, D_HD)
            o_p = _swa_prompt_attention(dq, dk, dv, d_sinks[e], bp, tp)
            vals_p = (ssm_p, conv_p, k_p[:, -WINDOW:], v_p[:, -WINDOW:])
            mix_s, vals_s = _odd_mixer([p[mp:] for p in parts], bs, ts, e, w, past)
            mix_groups = [[y_p, o_p], [mix_s]]
            for d, vals in ((st_p, vals_p), (st_s, vals_s)):
                d["c_ssm"].append(vals[0])
                d["c_conv"].append(vals[1])
                d["d_k"].append(vals[2])
                d["d_v"].append(vals[3])
            w_out = w_out_o_b[e]
        x = _post_mixer(mix_groups, x, w_out, g, w1_b[l], w2_b[l])
    sp = {n: jnp.stack(v, axis=0) for n, v in st_p.items()}
    ss = {n: jnp.stack(v, axis=0) for n, v in st_s.items()}
    y_prompt = x[:mp].reshape(bp, tp, D_MODEL)
    y_sample = x[mp:].reshape(bs, ts, D_MODEL)
    return (y_prompt, y_sample,
            sp["a_k"], ss["a_k"], sp["a_v"], ss["a_v"],
            sp["b_state"], ss["b_state"], sp["c_ssm"], ss["c_ssm"],
            sp["c_conv"], ss["c_conv"], sp["d_k"], ss["d_k"], sp["d_v"], ss["d_v"])
```

```python
import functools
import math

import jax
import jax.numpy as jnp
from jax import lax
from jax.experimental import pallas as pl
from jax.experimental.pallas import tpu as pltpu

F32 = jnp.float32
BF16 = jnp.bfloat16

D_MODEL = 1024
DEPTH = 4
PAGE_SIZE = 128
D_FF = 4 * D_MODEL
EPS = 1e-6
ROPE_THETA = 10000.0
MASK_NEG = -1e30
LB_FLOOR = 1e-30

A_HEADS = D_MODEL // 256
A_QK = 64
A_VD = 2 * A_QK
A_QKW = A_HEADS * 2 * A_QK
A_W = A_HEADS * A_VD
B_HEADS = 4
B_DK = D_MODEL // 8
B_DV = D_MODEL // 8
B_KW = B_HEADS * B_DK
B_W = B_HEADS * B_DV
C_HEADS = 16
C_HEADDIM = 64
C_DINNER = C_HEADS * C_HEADDIM
C_GROUPS = 2
C_DSTATE = 128
C_CONV = 4
C_CONV_DIM = C_DINNER + 2 * C_GROUPS * C_DSTATE
D_HEADS = 8
D_KV_HEADS = 2
D_GROUP = D_HEADS // D_KV_HEADS
D_HD = 64
WINDOW = 128

LANES = 128
ROPE_HALF = 32
DT_PAD = LANES
VMEM_LIMIT = 56 * 1024 * 1024
ROW_TILE = 512


def _rms(x, g):
    return x * lax.rsqrt(jnp.mean(x * x, axis=-1, keepdims=True) + EPS) * g


def _rope_tile(y, cos, sin_lo, sin_hi):
    up = pltpu.roll(y, LANES - ROPE_HALF, 1)
    down = pltpu.roll(y, ROPE_HALF, 1)
    return y * cos + up * sin_lo + down * sin_hi


def _proj_kernel(x_ref, g_ref, w_ref, cos_ref, slo_ref, shi_ref, *out_refs, widths, rope_flags):
    h = _rms(x_ref[...], g_ref[...]).astype(BF16)
    cos = cos_ref[...]
    slo = slo_ref[...]
    shi = shi_ref[...]
    off = 0
    for o_ref, width, roped in zip(out_refs, widths, rope_flags):
        y = jnp.dot(h, w_ref[:, off:off + width], preferred_element_type=F32)
        if roped:
            for c in range(0, width, LANES):
                o_ref[:, c:c + LANES] = _rope_tile(y[:, c:c + LANES], cos, slo, shi)
        else:
            o_ref[...] = y
        off += width


def _in_projection(x, g, w, rope_tabs, widths, rope_flags):
    m = x.shape[0]
    n = w.shape[1]
    assert sum(widths) == n and m % ROW_TILE == 0
    row = lambda i: (i, 0)
    const = lambda i: (0, 0)
    kern = functools.partial(_proj_kernel, widths=tuple(widths), rope_flags=tuple(rope_flags))
    return pl.pallas_call(
        kern,
        grid=(m // ROW_TILE,),
        in_specs=[
            pl.BlockSpec((ROW_TILE, D_MODEL), row),
            pl.BlockSpec((1, D_MODEL), const),
            pl.BlockSpec((D_MODEL, n), const),
            pl.BlockSpec((ROW_TILE, LANES), row),
            pl.BlockSpec((ROW_TILE, LANES), row),
            pl.BlockSpec((ROW_TILE, LANES), row),
        ],
        out_specs=[pl.BlockSpec((ROW_TILE, wd), row) for wd in widths],
        out_shape=[jax.ShapeDtypeStruct((m, wd), F32) for wd in widths],
        compiler_params=pltpu.CompilerParams(dimension_semantics=("arbitrary",), vmem_limit_bytes=VMEM_LIMIT),
        name="in_projection",
    )(x, g, w, *rope_tabs)


FF_TILE = 1024


def _post_kernel(*refs, group_sizes, group_tiles):
    n_pieces = sum(group_sizes)
    piece_refs = refs[:n_pieces]
    x_ref, wo_ref, g_ref, w1_ref, w2_ref, o_ref, x1_s, h_s, acc_s = refs[n_pieces:]
    i = pl.program_id(0)
    j = pl.program_id(1)

    def start_tile(pieces):
        m = None
        off = 0
        for p_ref in pieces:
            width = p_ref.shape[1]
            d = jnp.dot(p_ref[...].astype(BF16), wo_ref[off:off + width, :], preferred_element_type=F32)
            m = d if m is None else m + d
            off += width
        x1 = x_ref[...] + _rms(m, g_ref[1:2, :])
        x1_s[...] = x1
        h_s[...] = _rms(x1, g_ref[2:3, :]).astype(BF16)
        acc_s[...] = jnp.zeros_like(acc_s)

    first = 0
    pos = 0
    for size, tiles in zip(group_sizes, group_tiles):
        pieces = piece_refs[pos:pos + size]

        @pl.when((j == 0) & (i >= first) & (i < first + tiles))
        def _(pieces=pieces):
            start_tile(pieces)

        first += tiles
        pos += size

    a = jnp.dot(h_s[...], w1_ref[...], preferred_element_type=F32)
    a = jnp.square(jnp.maximum(a, 0.0)).astype(BF16)
    acc_s[...] += jnp.dot(a, w2_ref[...], preferred_element_type=F32)

    @pl.when(j == pl.num_programs(1) - 1)
    def _():
        o_ref[...] = x1_s[...] + _rms(acc_s[...], g_ref[3:4, :])


def _post_mixer(mix_groups, x, w_out, g, w1, w2):
    m = x.shape[0]
    kmix = w_out.shape[0]
    piece_specs = []
    pieces = []
    first = 0
    group_tiles = []
    for group in mix_groups:
        rows = group[0].shape[0]
        assert rows % ROW_TILE == 0 and sum(p.shape[1] for p in group) == kmix
        tiles = rows // ROW_TILE
        for p in group:
            assert p.shape[0] == rows
            piece_specs.append(pl.BlockSpec(
                (ROW_TILE, p.shape[1]), lambda i, j, first=first, tiles=tiles: (jnp.clip(i - first, 0, tiles - 1), 0)))
            pieces.append(p)
        group_tiles.append(tiles)
        first += tiles
    assert first * ROW_TILE == m
    kern = functools.partial(_post_kernel, group_sizes=tuple(len(grp) for grp in mix_groups),
                             group_tiles=tuple(group_tiles))
    return pl.pallas_call(
        kern,
        grid=(m // ROW_TILE, D_FF // FF_TILE),
        in_specs=piece_specs + [
            pl.BlockSpec((ROW_TILE, D_MODEL), lambda i, j: (i, 0)),
            pl.BlockSpec((kmix, D_MODEL), lambda i, j: (0, 0)),
            pl.BlockSpec((4, D_MODEL), lambda i, j: (0, 0)),
            pl.BlockSpec((D_MODEL, FF_TILE), lambda i, j: (0, j)),
            pl.BlockSpec((FF_TILE, D_MODEL), lambda i, j: (j, 0)),
        ],
        out_specs=pl.BlockSpec((ROW_TILE, D_MODEL), lambda i, j: (i, 0)),
        out_shape=jax.ShapeDtypeStruct((m, D_MODEL), F32),
        scratch_shapes=[
            pltpu.VMEM((ROW_TILE, D_MODEL), F32),
            pltpu.VMEM((ROW_TILE, D_MODEL), BF16),
            pltpu.VMEM((ROW_TILE, D_MODEL), F32),
        ],
        compiler_params=pltpu.CompilerParams(dimension_semantics=("arbitrary", "arbitrary"),
                                             vmem_limit_bytes=VMEM_LIMIT),
        name="post_mixer_mlp",
    )(*pieces, x, w_out, g, w1, w2)


DEC_PAGES = 8
A_ROWS_PER_HEAD = 2
NT_DIMS = (((1,), (1,)), ((), ()))
TN_DIMS = (((0,), (0,)), ((), ()))


def _decode_kernel(pt_ref, q_ref, kn_ref, vn_ref, sub_ref, lam_ref, *rest, n_pages, t_new, out_scale):
    del pt_ref
    k_refs = rest[:n_pages]
    v_refs = rest[n_pages:2 * n_pages]
    o_ref = rest[2 * n_pages]
    qhat_s, m_s, l_s, acc_s = rest[2 * n_pages + 1:]
    j = pl.program_id(1)
    rph = A_ROWS_PER_HEAD * t_new
    rows = A_HEADS * rph

    def page_tokens(ref):
        return jnp.concatenate([ref[pl.ds(h, PAGE_SIZE, stride=A_HEADS), :] for h in range(A_HEADS)],
                               axis=1).astype(BF16)

    def head_diag(pv):
        return jnp.concatenate([pv[h * rph:(h + 1) * rph, h * LANES:(h + 1) * LANES] for h in range(A_HEADS)], axis=0)

    @pl.when(j == 0)
    def _():
        qhat_s[...] = jnp.zeros(qhat_s.shape, BF16)
        qt = q_ref[...] * (A_QK ** -0.5)
        lane = lax.broadcasted_iota(jnp.int32, (t_new, LANES), 1)
        r = lax.broadcasted_iota(jnp.int32, (rph, PAGE_SIZE), 0)
        c = lax.broadcasted_iota(jnp.int32, (rph, PAGE_SIZE), 1)
        pad = jnp.zeros((PAGE_SIZE - t_new, LANES), F32)
        for h in range(A_HEADS):
            qh = qt[:, h * LANES:(h + 1) * LANES]
            qhat = jnp.concatenate([jnp.where(lane < A_QK, qh, 0.0), jnp.where(lane < A_QK, 0.0, qh)],
                                   axis=0).astype(BF16)
            qhat_s[h * rph:(h + 1) * rph, h * LANES:(h + 1) * LANES] = qhat
            kn = jnp.concatenate([kn_ref[:, h * LANES:(h + 1) * LANES], pad], axis=0).astype(BF16)
            vn = jnp.concatenate([vn_ref[:, h * LANES:(h + 1) * LANES], pad], axis=0).astype(BF16)
            s = lax.dot_general(qhat, kn, NT_DIMS, preferred_element_type=F32)
            s = jnp.where(c <= r % t_new, s, MASK_NEG)
            m = jnp.max(s, axis=-1, keepdims=True)
            p = jnp.exp(s - m)
            m_s[h * rph:(h + 1) * rph, :] = m
            l_s[h * rph:(h + 1) * rph, :] = jnp.sum(p, axis=-1, keepdims=True)
            acc_s[h * rph:(h + 1) * rph, :] = jnp.dot(p.astype(BF16), vn, preferred_element_type=F32)

    qhat_all = qhat_s[...]
    ss = [lax.dot_general(qhat_all, page_tokens(k_refs[n]), NT_DIMS, preferred_element_type=F32)
          for n in range(n_pages)]
    smax = functools.reduce(jnp.maximum, ss)
    m_old = m_s[...]
    m_new = jnp.maximum(m_old, jnp.max(smax, axis=-1, keepdims=True))
    alpha = jnp.exp(m_old - m_new)
    ps = [jnp.exp(s - m_new) for s in ss]
    l_s[...] = alpha * l_s[...] + jnp.sum(functools.reduce(jnp.add, ps), axis=-1, keepdims=True)
    pv = None
    for n in range(n_pages):
        d = jnp.dot(ps[n].astype(BF16), page_tokens(v_refs[n]), preferred_element_type=F32)
        pv = d if pv is None else pv + d
    acc_s[...] = alpha * acc_s[...] + head_diag(pv)
    m_s[...] = m_new

    @pl.when(j == pl.num_programs(1) - 1)
    def _():
        out = acc_s[...] / l_s[...]
        lam = lam_ref[...]
        for h in range(A_HEADS):
            o_h = out[h * rph:h * rph + t_new] - lam * out[h * rph + t_new:(h + 1) * rph]
            o_ref[:, h * LANES:(h + 1) * LANES] = _rms(o_h, sub_ref[...]) * out_scale


def _paged_diff_attention(q, k_new, v_new, cache_k, cache_v, page_table, layer_e, lam, subln, out_scale):
    b, t_new, _ = q.shape
    n_table = page_table.shape[1]
    assert n_table % DEC_PAGES == 0
    seq = lambda i, j, pt: (i, 0, 0)
    const = lambda i, j, pt: (0, 0)

    def page_spec(n):
        return pl.BlockSpec((None, None, PAGE_SIZE * A_HEADS, A_VD),
                            lambda i, j, pt: (layer_e, pt[i, j * DEC_PAGES + n], 0, 0))

    rows = A_HEADS * A_ROWS_PER_HEAD * t_new
    kern = functools.partial(_decode_kernel, n_pages=DEC_PAGES, t_new=t_new, out_scale=out_scale)
    grid_spec = pltpu.PrefetchScalarGridSpec(
        num_scalar_prefetch=1,
        grid=(b, n_table // DEC_PAGES),
        in_specs=[pl.BlockSpec((None, t_new, A_QKW), seq),
                  pl.BlockSpec((None, t_new, A_QKW), seq),
                  pl.BlockSpec((None, t_new, A_W), seq),
                  pl.BlockSpec((1, A_VD), const),
                  pl.BlockSpec((1, A_VD), const)]
                 + [page_spec(n) for n in range(DEC_PAGES)] * 2,
        out_specs=pl.BlockSpec((None, t_new, A_W), seq),
        scratch_shapes=[pltpu.VMEM((rows, A_QKW), BF16),
                        pltpu.VMEM((rows, 1), F32),
                        pltpu.VMEM((rows, 1), F32),
                        pltpu.VMEM((rows, A_VD), F32)],
    )
    lam_row = jnp.full((1, A_VD), lam, F32)
    return pl.pallas_call(
        kern,
        grid_spec=grid_spec,
        out_shape=jax.ShapeDtypeStruct((b, t_new, A_W), F32),
        compiler_params=pltpu.CompilerParams(dimension_semantics=("arbitrary", "arbitrary"),
                                             vmem_limit_bytes=VMEM_LIMIT),
        name="paged_diff_attention",
    )(page_table, q, k_new, v_new, subln.reshape(1, A_VD), lam_row,
      *([cache_k] * DEC_PAGES), *([cache_v] * DEC_PAGES))


ATT_TILE = 512


def _flash_kernel(q_ref, k_ref, v_ref, sub_ref, lam_ref, o_ref, qhat_s, m_s, l_s, acc_s, *, out_scale):
    qi = pl.program_id(2)
    ki = pl.program_id(3)
    tq = q_ref.shape[0]
    tk = k_ref.shape[0]

    @pl.when(ki == 0)
    def _():
        qt = q_ref[...] * (A_QK ** -0.5)
        lane = lax.broadcasted_iota(jnp.int32, (tq, LANES), 1)
        qhat_s[0:tq, :] = jnp.where(lane < A_QK, qt, 0.0).astype(BF16)
        qhat_s[tq:2 * tq, :] = jnp.where(lane < A_QK, 0.0, qt).astype(BF16)
        m_s[...] = jnp.full(m_s.shape, MASK_NEG, F32)
        l_s[...] = jnp.zeros(l_s.shape, F32)
        acc_s[...] = jnp.zeros(acc_s.shape, F32)

    def step(on_diagonal):
        s = lax.dot_general(qhat_s[...], k_ref[...].astype(BF16), NT_DIMS, preferred_element_type=F32)
        if on_diagonal:
            r = lax.broadcasted_iota(jnp.int32, (2 * tq, tk), 0)
            c = lax.broadcasted_iota(jnp.int32, (2 * tq, tk), 1)
            s = jnp.where(c <= jnp.where(r >= tq, r - tq, r), s, MASK_NEG)
        m_old = m_s[...]
        m_new = jnp.maximum(m_old, jnp.max(s, axis=-1, keepdims=True))
        alpha = jnp.exp(m_old - m_new)
        p = jnp.exp(s - m_new)
        l_s[...] = alpha * l_s[...] + jnp.sum(p, axis=-1, keepdims=True)
        acc_s[...] = alpha * acc_s[...] + jnp.dot(p.astype(BF16), v_ref[...].astype(BF16),
                                                  preferred_element_type=F32)
        m_s[...] = m_new

    @pl.when(ki < qi)
    def _():
        step(False)

    @pl.when(ki == qi)
    def _():
        step(True)
        out = acc_s[...] / l_s[...]
        o = out[0:tq] - lam_ref[...] * out[tq:2 * tq]
        o_ref[...] = _rms(o, sub_ref[...]) * out_scale


def _flash_diff_attention(q, k, v, n_batch, seq_len, lam, subln, out_scale):
    assert seq_len % ATT_TILE == 0
    nt = seq_len // ATT_TILE
    qmap = lambda b, h, qi, ki: (b * nt + qi, h)
    kmap = lambda b, h, qi, ki: (b * nt + jnp.minimum(ki, qi), h)
    const = lambda b, h, qi, ki: (0, 0)
    kern = functools.partial(_flash_kernel, out_scale=out_scale)
    return pl.pallas_call(
        kern,
        grid=(n_batch, A_HEADS, nt, nt),
        in_specs=[pl.BlockSpec((ATT_TILE, LANES), qmap),
                  pl.BlockSpec((ATT_TILE, LANES), kmap),
                  pl.BlockSpec((ATT_TILE, LANES), kmap),
                  pl.BlockSpec((1, A_VD), const),
                  pl.BlockSpec((1, A_VD), const)],
        out_specs=pl.BlockSpec((ATT_TILE, LANES), qmap),
        out_shape=jax.ShapeDtypeStruct((n_batch * seq_len, A_W), F32),
        scratch_shapes=[pltpu.VMEM((2 * ATT_TILE, LANES), BF16),
                        pltpu.VMEM((2 * ATT_TILE, 1), F32),
                        pltpu.VMEM((2 * ATT_TILE, 1), F32),
                        pltpu.VMEM((2 * ATT_TILE, A_VD), F32)],
        compiler_params=pltpu.CompilerParams(dimension_semantics=("arbitrary",) * 4, vmem_limit_bytes=VMEM_LIMIT),
        name="flash_diff_attention",
    )(q, k, v, subln.reshape(1, A_VD), jnp.full((1, A_VD), lam, F32))


SUBLANES = 8
HGRN_CHUNK = 32
HGRN_BLOCK = 512


def _split3_dot(tri, x):
    hi = x.astype(BF16)
    r1 = x - hi.astype(F32)
    mid = r1.astype(BF16)
    lo = (r1 - mid.astype(F32)).astype(BF16)
    return (jnp.dot(tri, hi, preferred_element_type=F32) + jnp.dot(tri, mid, preferred_element_type=F32)
            + jnp.dot(tri, lo, preferred_element_type=F32))


def _cumsum_rows(x, tri):
    n = x.shape[0]
    if n > SUBLANES:
        return _split3_dot(tri, x)
    row = lax.broadcasted_iota(jnp.int32, x.shape, 0)
    shift = 1
    while shift < n:
        x = x + jnp.where(row >= shift, pltpu.roll(x, shift, 0), 0.0)
        shift *= 2
    return x


def _hgrn2_kernel(*refs, chunk, has_s0, n_seq):
    if has_s0:
        q_ref, f_ref, i_ref, g_ref, lb_ref, nw_ref, s0_ref, o_ref, sout_ref, st_s = refs
    else:
        q_ref, f_ref, i_ref, g_ref, lb_ref, nw_ref, o_ref, sout_ref, st_s = refs
        s0_ref = None
    tb = pl.program_id(1)
    n_chunks = q_ref.shape[0] // chunk
    n_tiles = chunk // SUBLANES

    def load_state(i):
        for h in range(B_HEADS):
            st_s[h] = s0_ref[i, h].T if has_s0 else jnp.zeros((B_DV, B_DK), F32)

    def store_state(i):
        for h in range(B_HEADS):
            sout_ref[i, h] = st_s[h].T

    nw = nw_ref[...]
    tri = (lax.broadcasted_iota(jnp.int32, (chunk, chunk), 0)
           >= lax.broadcasted_iota(jnp.int32, (chunk, chunk), 1)).astype(BF16)
    row8 = lax.broadcasted_iota(jnp.int32, (SUBLANES, 1), 0)
    mxu = (lambda a: a.astype(BF16)) if chunk >= 2 * SUBLANES else (lambda a: a)

    def head_chunk(r0, h):
        cols = slice(h * B_DK, (h + 1) * B_DK)
        lb = lb_ref[:, cols]
        log_lb = jnp.log(jnp.maximum(lb, LB_FLOOR))
        z = f_ref[pl.ds(r0, chunk), cols]
        qq = q_ref[pl.ds(r0, chunk), cols]
        vv = i_ref[pl.ds(r0, chunk), cols]
        gg = g_ref[pl.ds(r0, chunk), cols]
        log_sig = jnp.minimum(z, 0.0) - jnp.log1p(jnp.exp(-jnp.abs(z)))
        arg = jnp.log1p(-lb) + log_sig
        log_f = jnp.maximum(log_lb, arg) + jnp.log1p(jnp.exp(-jnp.abs(log_lb - arg)))
        kk = (1.0 - lb) * jax.nn.sigmoid(-z)
        qf = qq * jax.nn.sigmoid(qq)
        gcum = _cumsum_rows(log_f, tri)
        g_last = gcum[chunk - 1:chunk, :]
        st = st_s[h]
        o_inter = lax.dot_general(mxu(qf * jnp.exp(gcum)), mxu(st), NT_DIMS, preferred_element_type=F32)
        g_t = [gcum[i * SUBLANES:(i + 1) * SUBLANES] for i in range(n_tiles)]
        q_t = [qf[i * SUBLANES:(i + 1) * SUBLANES] for i in range(n_tiles)]
        o_t = [None] * n_tiles
        for s in range(chunk):
            g_s = gcum[s:s + 1, :]
            k_s = kk[s:s + 1, :]
            v_s = vv[s:s + 1, :]
            for ti in range(s // SUBLANES, n_tiles):
                e = jnp.exp(jnp.minimum(g_t[ti] - g_s, 0.0))
                col = jnp.sum(q_t[ti] * (k_s * e), axis=-1, keepdims=True)
                if ti == s // SUBLANES:
                    col = jnp.where(row8 >= s % SUBLANES, col, 0.0)
                term = col * v_s
                o_t[ti] = term if o_t[ti] is None else o_t[ti] + term
        ob = o_inter + jnp.concatenate(o_t, axis=0)
        kw = kk * jnp.exp(g_last - gcum)
        st_s[h] = st * jnp.exp(g_last) + lax.dot_general(mxu(vv), mxu(kw), TN_DIMS, preferred_element_type=F32)
        o_ref[pl.ds(r0, chunk), cols] = _rms(ob, nw) * jax.nn.sigmoid(gg)

    def body(c, carry):
        r0 = c * chunk if isinstance(c, int) else pl.multiple_of(c * chunk, chunk)
        for h in range(B_HEADS):
            head_chunk(r0, h)
        return carry

    if n_seq == 1:
        @pl.when(tb == 0)
        def _():
            load_state(0)

        lax.fori_loop(0, n_chunks, body, 0)

        @pl.when(tb == pl.num_programs(1) - 1)
        def _():
            store_state(0)
    else:
        for i in range(n_seq):
            load_state(i)
            body(i, 0)
            store_state(i)


HGRN_SEQS_PER_STEP = 8


def _hgrn2(bq, bf, bi, bg, lb, b_norm_w, n_batch, seq_len, row_off, s0, layer_e):
    if seq_len <= HGRN_CHUNK:
        n_seq = math.gcd(n_batch, HGRN_SEQS_PER_STEP)
        chunk = seq_len
        blk = n_seq * seq_len
        nb = 1
    else:
        n_seq = 1
        blk = min(HGRN_BLOCK, seq_len)
        chunk = math.gcd(blk, HGRN_CHUNK)
        nb = seq_len // blk
    assert seq_len % chunk == 0 and chunk % SUBLANES == 0 and row_off % blk == 0
    off = row_off // blk
    rmap = lambda b, t: (off + b * nb + t, 0)
    omap = lambda b, t: (b * nb + t, 0)
    const = lambda b, t: (0, 0)
    in_specs = [pl.BlockSpec((blk, B_KW), rmap)] * 4 + [pl.BlockSpec((1, B_KW), const), pl.BlockSpec((1, B_DV), const)]
    args = [bq, bf, bi, bg, lb.reshape(1, B_KW), b_norm_w.reshape(1, B_DV)]
    if s0 is not None:
        in_specs.append(pl.BlockSpec((None, n_seq, B_HEADS, B_DK, B_DV), lambda b, t: (layer_e, b, 0, 0, 0)))
        args.append(s0)
    kern = functools.partial(_hgrn2_kernel, chunk=chunk, has_s0=s0 is not None, n_seq=n_seq)
    return pl.pallas_call(
        kern,
        grid=(n_batch // n_seq, nb),
        in_specs=in_specs,
        out_specs=[pl.BlockSpec((blk, B_W), omap),
                   pl.BlockSpec((n_seq, B_HEADS, B_DK, B_DV), lambda b, t: (b, 0, 0, 0))],
        out_shape=[jax.ShapeDtypeStruct((n_batch * seq_len, B_W), F32),
                   jax.ShapeDtypeStruct((n_batch, B_HEADS, B_DK, B_DV), F32)],
        scratch_shapes=[pltpu.VMEM((B_HEADS, B_DV, B_DK), F32)],
        compiler_params=pltpu.CompilerParams(dimension_semantics=("arbitrary",) * 2, vmem_limit_bytes=VMEM_LIMIT),
        name="hgrn2",
    )(*args)


SSD_CHUNK = 128
C_PAIRS = C_HEADS // 2
C_GROUP_W = C_DINNER // C_GROUPS


def _softplus(x):
    return jnp.maximum(x, 0.0) + jnp.log1p(jnp.exp(-jnp.abs(x)))


def _ssd_kernel(xbc_ref, z_ref, dt_ref, cw_ref, cb_ref, dtb_ref, alog_ref, cd_ref, nw_ref,
                y_ref, sout_ref, cout_ref, xpad_s, st_s):
    c = pl.program_id(1)
    L = xbc_ref.shape[0]
    tail = C_CONV - 1

    @pl.when(c == 0)
    def _():
        xpad_s[0:SUBLANES, :] = jnp.zeros((SUBLANES, C_CONV_DIM), F32)
        st_s[...] = jnp.zeros(st_s.shape, F32)

    xpad_s[SUBLANES:SUBLANES + L, :] = xbc_ref[...]
    acc = cb_ref[...] + cw_ref[0:1, :] * xpad_s[pl.ds(SUBLANES - tail, L), :]
    for j in range(1, C_CONV):
        acc = acc + cw_ref[j:j + 1, :] * xpad_s[pl.ds(SUBLANES - tail + j, L), :]
    xc = acc * jax.nn.sigmoid(acc)

    @pl.when(c == pl.num_programs(1) - 1)
    def _():
        cout_ref[...] = xpad_s[pl.ds(SUBLANES + L - tail, tail), :]

    xpad_s[0:SUBLANES, :] = xpad_s[L:L + SUBLANES, :]

    dtv = _softplus(dt_ref[...] + dtb_ref[...])
    a = dtv * (-jnp.exp(alog_ref[...]))
    row = lax.broadcasted_iota(jnp.int32, (L, L), 0)
    col = lax.broadcasted_iota(jnp.int32, (L, L), 1)
    causal = row >= col
    gcum = _split3_dot(causal.astype(BF16), a)
    gcum_t = gcum.T
    lane = lax.broadcasted_iota(jnp.int32, (L, LANES), 1)
    low = lane < C_HEADDIM
    srow = lax.broadcasted_iota(jnp.int32, (2 * C_HEADDIM, C_DSTATE), 0)

    for g in range(C_GROUPS):
        b_g = xc[:, C_DINNER + g * C_DSTATE:C_DINNER + (g + 1) * C_DSTATE].astype(BF16)
        c_g = xc[:, C_DINNER + (C_GROUPS + g) * C_DSTATE:C_DINNER + (C_GROUPS + g + 1) * C_DSTATE].astype(BF16)
        cb = lax.dot_general(c_g, b_g, NT_DIMS, preferred_element_type=F32)
        u_parts = []
        sq = None
        for jp in range(C_PAIRS // C_GROUPS):
            pair = g * (C_PAIRS // C_GROUPS) + jp
            x_pair = xc[:, pair * LANES:(pair + 1) * LANES]
            y_pair = None
            gb = []
            dtb = []
            for hh in range(2):
                h = 2 * pair + hh
                g_col = jnp.broadcast_to(gcum[:, h:h + 1], (L, LANES))
                dt_col = jnp.broadcast_to(dtv[:, h:h + 1], (L, LANES))
                gb.append(g_col)
                dtb.append(dt_col)
                dec = jnp.exp(jnp.minimum(g_col - gcum_t[h:h + 1, :], 0.0))
                m_h = jnp.where(causal, cb * dec, 0.0).astype(BF16)
                half = low if hh == 0 else jnp.logical_not(low)
                xdt = jnp.where(half, x_pair * dt_col, 0.0).astype(BF16)
                d = jnp.dot(m_h, xdt, preferred_element_type=F32)
                y_pair = d if y_pair is None else y_pair + d
            st = st_s[pair]
            y_inter = lax.dot_general(c_g, st.astype(BF16), NT_DIMS, preferred_element_type=F32)
            y_pair = y_pair + y_inter * jnp.where(low, jnp.exp(gb[0]), jnp.exp(gb[1]))
            y_pair = y_pair + cd_ref[:, pair * LANES:(pair + 1) * LANES] * x_pair
            gl = [gb[0][L - 1:L, :], gb[1][L - 1:L, :]]
            w_pair = jnp.where(low, jnp.exp(gl[0] - gb[0]) * dtb[0], jnp.exp(gl[1] - gb[1]) * dtb[1])
            decay = jnp.where(srow < C_HEADDIM, jnp.exp(gl[0]), jnp.exp(gl[1]))
            st_s[pair] = decay * st + lax.dot_general((x_pair * w_pair).astype(BF16), b_g, TN_DIMS,
                                                      preferred_element_type=F32)
            zz = z_ref[:, pair * LANES:(pair + 1) * LANES]
            u = y_pair * (zz * jax.nn.sigmoid(zz))
            u_parts.append(u)
            s2 = jnp.sum(u * u, axis=-1, keepdims=True)
            sq = s2 if sq is None else sq + s2
        inv = lax.rsqrt(sq * (1.0 / C_GROUP_W) + EPS)
        for jp, u in enumerate(u_parts):
            pair = g * (C_PAIRS // C_GROUPS) + jp
            y_ref[:, pair * LANES:(pair + 1) * LANES] = u * inv * nw_ref[:, pair * LANES:(pair + 1) * LANES]

    @pl.when(c == pl.num_programs(1) - 1)
    def _():
        sout_ref[...] = st_s[...]


def _ssd_prompt(xbc, z, dt, conv_w, conv_b, dt_bias, a_log, c_d, c_norm_w, n_batch, seq_len):
    assert seq_len % SSD_CHUNK == 0
    nc = seq_len // SSD_CHUNK
    rmap = lambda b, c: (b * nc + c, 0)
    const = lambda b, c: (0, 0)
    pad16 = lambda v: jnp.pad(v.astype(F32), (0, DT_PAD - C_HEADS)).reshape(1, DT_PAD)
    y, s_out, c_out = pl.pallas_call(
        _ssd_kernel,
        grid=(n_batch, nc),
        in_specs=[pl.BlockSpec((SSD_CHUNK, C_CONV_DIM), rmap),
                  pl.BlockSpec((SSD_CHUNK, C_DINNER), rmap),
                  pl.BlockSpec((SSD_CHUNK, DT_PAD), rmap),
                  pl.BlockSpec((C_CONV, C_CONV_DIM), const),
                  pl.BlockSpec((1, C_CONV_DIM), const),
                  pl.BlockSpec((1, DT_PAD), const),
                  pl.BlockSpec((1, DT_PAD), const),
                  pl.BlockSpec((1, C_DINNER), const),
                  pl.BlockSpec((1, C_DINNER), const)],
        out_specs=[pl.BlockSpec((SSD_CHUNK, C_DINNER), rmap),
                   pl.BlockSpec((None, C_PAIRS, 2 * C_HEADDIM, C_DSTATE), lambda b, c: (b, 0, 0, 0)),
                   pl.BlockSpec((None, C_CONV - 1, C_CONV_DIM), lambda b, c: (b, 0, 0))],
        out_shape=[jax.ShapeDtypeStruct((n_batch * seq_len, C_DINNER), F32),
                   jax.ShapeDtypeStruct((n_batch, C_PAIRS, 2 * C_HEADDIM, C_DSTATE), F32),
                   jax.ShapeDtypeStruct((n_batch, C_CONV - 1, C_CONV_DIM), F32)],
        scratch_shapes=[pltpu.VMEM((SUBLANES + SSD_CHUNK, C_CONV_DIM), F32),
                        pltpu.VMEM((C_PAIRS, 2 * C_HEADDIM, C_DSTATE), F32)],
        compiler_params=pltpu.CompilerParams(dimension_semantics=("arbitrary", "arbitrary"),
                                             vmem_limit_bytes=VMEM_LIMIT),
        name="ssd_prompt",
    )(xbc, z, dt, conv_w, conv_b.reshape(1, C_CONV_DIM), pad16(dt_bias), pad16(a_log),
      jnp.repeat(c_d.astype(F32), C_HEADDIM).reshape(1, C_DINNER), c_norm_w.reshape(1, C_DINNER))
    return y, s_out.reshape(n_batch, C_HEADS, C_HEADDIM, C_DSTATE), c_out


def _swa_kernel(q_ref, kp_ref, kc_ref, vp_ref, vc_ref, sink_ref, o_ref):
    i = pl.program_id(1)
    w = q_ref.shape[0]
    kk = jnp.concatenate([kp_ref[...], kc_ref[...]], axis=0).astype(BF16)
    vv = jnp.concatenate([vp_ref[...], vc_ref[...]], axis=0).astype(BF16)
    rows = D_GROUP * w
    r = lax.broadcasted_iota(jnp.int32, (rows, 2 * w), 0) % w
    c = lax.broadcasted_iota(jnp.int32, (rows, 2 * w), 1)
    first_key = jnp.where(i > 0, 0, w)
    visible = (c <= w + r) & (c > r) & (c >= first_key)
    lane = lax.broadcasted_iota(jnp.int32, (w, LANES), 1)
    low = lane < D_HD
    for g in range(D_KV_HEADS):
        g_half = low if g == 0 else jnp.logical_not(low)
        q_parts = []
        sink_parts = []
        for rr in range(D_GROUP):
            h = g * D_GROUP + rr
            pair, hh = divmod(h, 2)
            q_pair = q_ref[:, pair * LANES:(pair + 1) * LANES] * (D_HD ** -0.5)
            q_src = q_pair if hh == g else pltpu.roll(q_pair, D_HD, 1)
            q_parts.append(jnp.where(g_half, q_src, 0.0).astype(BF16))
            sink_parts.append(jnp.broadcast_to(sink_ref[h:h + 1, 0:1], (w, 1)))
        qg = jnp.concatenate(q_parts, axis=0)
        sink = jnp.concatenate(sink_parts, axis=0)
        s = lax.dot_general(qg, kk, NT_DIMS, preferred_element_type=F32)
        s = jnp.where(visible, s, MASK_NEG)
        m = jnp.maximum(jnp.max(s, axis=-1, keepdims=True), sink)
        p = jnp.exp(s - m)
        p = p / (jnp.sum(p, axis=-1, keepdims=True) + jnp.exp(sink - m))
        o = jnp.dot(p.astype(BF16), vv, preferred_element_type=F32)
        for jp in range(D_GROUP // 2):
            pair = g * (D_GROUP // 2) + jp
            halves = []
            for hh in range(2):
                o_h = o[(2 * jp + hh) * w:(2 * jp + hh + 1) * w]
                halves.append(o_h if hh == g else pltpu.roll(o_h, D_HD, 1))
            o_ref[:, pair * LANES:(pair + 1) * LANES] = jnp.where(low, halves[0], halves[1])


def _swa_prompt_attention(dq, dk, dv, sinks, n_batch, seq_len):
    assert seq_len % WINDOW == 0
    nb = seq_len // WINDOW
    cur = lambda b, i: (b * nb + i, 0)
    prev = lambda b, i: (b * nb + jnp.maximum(i - 1, 0), 0)
    kv_w = D_KV_HEADS * D_HD
    return pl.pallas_call(
        _swa_kernel,
        grid=(n_batch, nb),
        in_specs=[pl.BlockSpec((WINDOW, D_HEADS * D_HD), cur),
                  pl.BlockSpec((WINDOW, kv_w), prev),
                  pl.BlockSpec((WINDOW, kv_w), cur),
                  pl.BlockSpec((WINDOW, kv_w), prev),
                  pl.BlockSpec((WINDOW, kv_w), cur),
                  pl.BlockSpec((D_HEADS, LANES), lambda b, i: (0, 0))],
        out_specs=pl.BlockSpec((WINDOW, D_HEADS * D_HD), cur),
        out_shape=jax.ShapeDtypeStruct((n_batch * seq_len, D_HEADS * D_HD), F32),
        compiler_params=pltpu.CompilerParams(dimension_semantics=("arbitrary", "arbitrary"),
                                             vmem_limit_bytes=VMEM_LIMIT),
        name="swa_prompt",
    )(dq, dk, dk, dv, dv, jnp.broadcast_to(sinks.astype(F32)[:, None], (D_HEADS, LANES)))


SAMPLE_SSD_SEQS = 16
SAMPLE_SWA_SEQS = 8


def _ssd_sample_kernel(xbc_ref, z_ref, dt_ref, c0_ref, s0_ref, cw_ref, cb_ref, dtb_ref, alog_ref, cd_ref, nw_ref,
                       y_ref, sout_ref, cout_ref, xpad_s, xc_s, *, t_len):
    L = xbc_ref.shape[0]
    n_seq = L // t_len
    tail = C_CONV - 1
    for i in range(n_seq):
        xpad_s[pl.ds(SUBLANES - tail, tail), :] = c0_ref[i]
        xpad_s[SUBLANES:SUBLANES + t_len, :] = xbc_ref[i * t_len:(i + 1) * t_len, :]
        acc = cb_ref[...] + cw_ref[0:1, :] * xpad_s[pl.ds(SUBLANES - tail, t_len), :]
        for j in range(1, C_CONV):
            acc = acc + cw_ref[j:j + 1, :] * xpad_s[pl.ds(SUBLANES - tail + j, t_len), :]
        xc_s[i * t_len:(i + 1) * t_len, :] = acc * jax.nn.sigmoid(acc)
        cout_ref[i] = xpad_s[pl.ds(SUBLANES + t_len - tail, tail), :]
    xc = xc_s[...]

    dtv = _softplus(dt_ref[...] + dtb_ref[...])
    a = dtv * (-jnp.exp(alog_ref[...]))
    row = lax.broadcasted_iota(jnp.int32, (L, L), 0)
    col = lax.broadcasted_iota(jnp.int32, (L, L), 1)
    same = (row // t_len) == (col // t_len)
    causal = (row >= col) & same
    gcum = _split3_dot(causal.astype(BF16), a)
    gtot = _split3_dot(same.astype(BF16), a)
    gcum_t = gcum.T
    lane = lax.broadcasted_iota(jnp.int32, (L, LANES), 1)
    low = lane < C_HEADDIM
    srow = lax.broadcasted_iota(jnp.int32, (2 * C_HEADDIM, C_DSTATE), 0)

    for g in range(C_GROUPS):
        b_f = xc[:, C_DINNER + g * C_DSTATE:C_DINNER + (g + 1) * C_DSTATE]
        c_f = xc[:, C_DINNER + (C_GROUPS + g) * C_DSTATE:C_DINNER + (C_GROUPS + g + 1) * C_DSTATE]
        b_g = b_f.astype(BF16)
        c_g = c_f.astype(BF16)
        cb = lax.dot_general(c_g, b_g, NT_DIMS, preferred_element_type=F32)
        u_parts = []
        sq = None
        for jp in range(C_PAIRS // C_GROUPS):
            pair = g * (C_PAIRS // C_GROUPS) + jp
            x_pair = xc[:, pair * LANES:(pair + 1) * LANES]
            y_pair = None
            gb, tb_, dtb = [], [], []
            for hh in range(2):
                h = 2 * pair + hh
                g_col = jnp.broadcast_to(gcum[:, h:h + 1], (L, LANES))
                dt_col = jnp.broadcast_to(dtv[:, h:h + 1], (L, LANES))
                gb.append(g_col)
                tb_.append(jnp.broadcast_to(gtot[:, h:h + 1], (L, LANES)))
                dtb.append(dt_col)
                dec = jnp.exp(jnp.minimum(g_col - gcum_t[h:h + 1, :], 0.0))
                m_h = jnp.where(causal, cb * dec, 0.0).astype(BF16)
                half = low if hh == 0 else jnp.logical_not(low)
                xdt = jnp.where(half, x_pair * dt_col, 0.0).astype(BF16)
                d = jnp.dot(m_h, xdt, preferred_element_type=F32)
                y_pair = d if y_pair is None else y_pair + d
            xw = x_pair * jnp.where(low, jnp.exp(tb_[0] - gb[0]) * dtb[0], jnp.exp(tb_[1] - gb[1]) * dtb[1])
            y_inter = []
            for i in range(n_seq):
                rows = slice(i * t_len, (i + 1) * t_len)
                st = s0_ref[i, pair]
                y_inter.append(lax.dot_general(c_f[rows], st, NT_DIMS, preferred_element_type=F32))
                decay = jnp.where(srow < C_HEADDIM, jnp.exp(tb_[0][i * t_len:i * t_len + 1, :]),
                                  jnp.exp(tb_[1][i * t_len:i * t_len + 1, :]))
                sout_ref[i, pair] = decay * st + lax.dot_general(xw[rows], b_f[rows], TN_DIMS,
                                                                 preferred_element_type=F32)
            y_pair = y_pair + jnp.concatenate(y_inter, axis=0) * jnp.where(low, jnp.exp(gb[0]), jnp.exp(gb[1]))
            y_pair = y_pair + cd_ref[:, pair * LANES:(pair + 1) * LANES] * x_pair
            zz = z_ref[:, pair * LANES:(pair + 1) * LANES]
            u = y_pair * (zz * jax.nn.sigmoid(zz))
            u_parts.append(u)
            s2 = jnp.sum(u * u, axis=-1, keepdims=True)
            sq = s2 if sq is None else sq + s2
        inv = lax.rsqrt(sq * (1.0 / C_GROUP_W) + EPS)
        for jp, u in enumerate(u_parts):
            pair = g * (C_PAIRS // C_GROUPS) + jp
            y_ref[:, pair * LANES:(pair + 1) * LANES] = u * inv * nw_ref[:, pair * LANES:(pair + 1) * LANES]


def _ssd_sample(xbc, z, dt, conv0, s0, layer_e, conv_w, conv_b, dt_bias, a_log, c_d, c_norm_w, n_batch, t_len, row_off):
    blk = SAMPLE_SSD_SEQS * t_len
    assert n_batch % SAMPLE_SSD_SEQS == 0 and row_off % blk == 0 and t_len == SUBLANES and blk == SSD_CHUNK
    off = row_off // blk
    rmap = lambda b: (off + b, 0)
    omap = lambda b: (b, 0)
    const = lambda b: (0, 0)
    pad16 = lambda v: jnp.pad(v.astype(F32), (0, DT_PAD - C_HEADS)).reshape(1, DT_PAD)
    s0_pairs = s0.reshape(s0.shape[0], n_batch, C_PAIRS, 2 * C_HEADDIM, C_DSTATE)
    kern = functools.partial(_ssd_sample_kernel, t_len=t_len)
    y, s_out, c_out = pl.pallas_call(
        kern,
        grid=(n_batch // SAMPLE_SSD_SEQS,),
        in_specs=[pl.BlockSpec((blk, C_CONV_DIM), rmap),
                  pl.BlockSpec((blk, C_DINNER), rmap),
                  pl.BlockSpec((blk, DT_PAD), rmap),
                  pl.BlockSpec((None, SAMPLE_SSD_SEQS, C_CONV - 1, C_CONV_DIM), lambda b: (layer_e, b, 0, 0)),
                  pl.BlockSpec((None, SAMPLE_SSD_SEQS, C_PAIRS, 2 * C_HEADDIM, C_DSTATE),
                               lambda b: (layer_e, b, 0, 0, 0)),
                  pl.BlockSpec((C_CONV, C_CONV_DIM), const),
                  pl.BlockSpec((1, C_CONV_DIM), const),
                  pl.BlockSpec((1, DT_PAD), const),
                  pl.BlockSpec((1, DT_PAD), const),
                  pl.BlockSpec((1, C_DINNER), const),
                  pl.BlockSpec((1, C_DINNER), const)],
        out_specs=[pl.BlockSpec((blk, C_DINNER), omap),
                   pl.BlockSpec((SAMPLE_SSD_SEQS, C_PAIRS, 2 * C_HEADDIM, C_DSTATE), lambda b: (b, 0, 0, 0)),
                   pl.BlockSpec((SAMPLE_SSD_SEQS, C_CONV - 1, C_CONV_DIM), lambda b: (b, 0, 0))],
        out_shape=[jax.ShapeDtypeStruct((n_batch * t_len, C_DINNER), F32),
                   jax.ShapeDtypeStruct((n_batch, C_PAIRS, 2 * C_HEADDIM, C_DSTATE), F32),
                   jax.ShapeDtypeStruct((n_batch, C_CONV - 1, C_CONV_DIM), F32)],
        scratch_shapes=[pltpu.VMEM((2 * SUBLANES, C_CONV_DIM), F32),
                        pltpu.VMEM((blk, C_CONV_DIM), F32)],
        compiler_params=pltpu.CompilerParams(dimension_semantics=("arbitrary",), vmem_limit_bytes=VMEM_LIMIT),
        name="ssd_sample",
    )(xbc, z, dt, conv0, s0_pairs, conv_w, conv_b.reshape(1, C_CONV_DIM), pad16(dt_bias), pad16(a_log),
      jnp.repeat(c_d.astype(F32), C_HEADDIM).reshape(1, C_DINNER), c_norm_w.reshape(1, C_DINNER))
    return y, s_out.reshape(n_batch, C_HEADS, C_HEADDIM, C_DSTATE), c_out


def _swa_sample_kernel(q_ref, kn_ref, vn_ref, kc_ref, vc_ref, sink_ref, o_ref, kw_ref, vw_ref, *, t_len):
    n_seq = q_ref.shape[0] // t_len
    rows = D_GROUP * t_len
    r = lax.broadcasted_iota(jnp.int32, (rows, WINDOW), 0) % t_len
    c = lax.broadcasted_iota(jnp.int32, (rows, WINDOW), 1)
    cache_visible = c > r
    new_visible = c <= r
    lane = lax.broadcasted_iota(jnp.int32, (t_len, LANES), 1)
    low = lane < D_HD
    pad = jnp.zeros((WINDOW - t_len, LANES), F32)
    for i in range(n_seq):
        tok = slice(i * t_len, (i + 1) * t_len)
        kc = kc_ref[i]
        vc = vc_ref[i]
        kn = kn_ref[tok, :]
        vn = vn_ref[tok, :]
        kw_ref[i, 0:WINDOW - t_len, :] = kc[t_len:WINDOW]
        kw_ref[i, WINDOW - t_len:WINDOW, :] = kn
        vw_ref[i, 0:WINDOW - t_len, :] = vc[t_len:WINDOW]
        vw_ref[i, WINDOW - t_len:WINDOW, :] = vn
        kc_b = kc.astype(BF16)
        vc_b = vc.astype(BF16)
        kn_b = jnp.concatenate([kn, pad], axis=0).astype(BF16)
        vn_b = jnp.concatenate([vn, pad], axis=0).astype(BF16)
        for g in range(D_KV_HEADS):
            g_half = low if g == 0 else jnp.logical_not(low)
            q_parts = []
            sink_parts = []
            for rr in range(D_GROUP):
                h = g * D_GROUP + rr
                pair, hh = divmod(h, 2)
                q_pair = q_ref[tok, pair * LANES:(pair + 1) * LANES] * (D_HD ** -0.5)
                q_src = q_pair if hh == g else pltpu.roll(q_pair, D_HD, 1)
                q_parts.append(jnp.where(g_half, q_src, 0.0))
                sink_parts.append(jnp.broadcast_to(sink_ref[h:h + 1, 0:1], (t_len, 1)))
            qg = jnp.concatenate(q_parts, axis=0).astype(BF16)
            sink = jnp.concatenate(sink_parts, axis=0)
            s_c = jnp.where(cache_visible, lax.dot_general(qg, kc_b, NT_DIMS, preferred_element_type=F32), MASK_NEG)
            s_n = jnp.where(new_visible, lax.dot_general(qg, kn_b, NT_DIMS, preferred_element_type=F32), MASK_NEG)
            m = jnp.maximum(jnp.max(jnp.maximum(s_c, s_n), axis=-1, keepdims=True), sink)
            p_c = jnp.exp(s_c - m)
            p_n = jnp.exp(s_n - m)
            denom = jnp.sum(p_c + p_n, axis=-1, keepdims=True) + jnp.exp(sink - m)
            o = (jnp.dot(p_c.astype(BF16), vc_b, preferred_element_type=F32)
                 + jnp.dot(p_n.astype(BF16), vn_b, preferred_element_type=F32)) / denom
            for jp in range(D_GROUP // 2):
                pair = g * (D_GROUP // 2) + jp
                halves = []
                for hh in range(2):
                    o_h = o[(2 * jp + hh) * t_len:(2 * jp + hh + 1) * t_len]
                    halves.append(o_h if hh == g else pltpu.roll(o_h, D_HD, 1))
                o_ref[tok, pair * LANES:(pair + 1) * LANES] = jnp.where(low, halves[0], halves[1])


def _swa_sample_attention(dq, dk, dv, cache_k, cache_v, sinks, n_batch, t_len, row_off):
    blk = SAMPLE_SWA_SEQS * t_len
    assert n_batch % SAMPLE_SWA_SEQS == 0 and row_off % blk == 0 and t_len == SUBLANES
    off = row_off // blk
    rmap = lambda b: (off + b, 0)
    kv_w = D_KV_HEADS * D_HD
    win = pl.BlockSpec((SAMPLE_SWA_SEQS, WINDOW, kv_w), lambda b: (b, 0, 0))
    kern = functools.partial(_swa_sample_kernel, t_len=t_len)
    return pl.pallas_call(
        kern,
        grid=(n_batch // SAMPLE_SWA_SEQS,),
        in_specs=[pl.BlockSpec((blk, D_HEADS * D_HD), rmap),
                  pl.BlockSpec((blk, kv_w), rmap),
                  pl.BlockSpec((blk, kv_w), rmap),
                  win, win,
                  pl.BlockSpec((D_HEADS, LANES), lambda b: (0, 0))],
        out_specs=[pl.BlockSpec((blk, D_HEADS * D_HD), lambda b: (b, 0)), win, win],
        out_shape=[jax.ShapeDtypeStruct((n_batch * t_len, D_HEADS * D_HD), F32),
                   jax.ShapeDtypeStruct((n_batch, WINDOW, kv_w), F32),
                   jax.ShapeDtypeStruct((n_batch, WINDOW, kv_w), F32)],
        compiler_params=pltpu.CompilerParams(dimension_semantics=("arbitrary",), vmem_limit_bytes=VMEM_LIMIT),
        name="swa_sample",
    )(dq, dk, dv, cache_k, cache_v, jnp.broadcast_to(sinks.astype(F32)[:, None], (D_HEADS, LANES)))


def _diff_lambda(lam, layer):
    lam = lam.astype(F32)
    lam_init = 0.8 - 0.6 * math.exp(-0.3 * layer)
    lam_full = jnp.exp(jnp.sum(lam[0] * lam[1])) - jnp.exp(jnp.sum(lam[2] * lam[3])) + lam_init
    return lam_full, lam_init


def _hgrn_lower_bound(p):
    sm = jax.nn.softmax(p.astype(F32), axis=0)
    return jnp.concatenate([jnp.zeros_like(sm[:1]), jnp.cumsum(sm[1:], axis=0)], axis=0)


def _rope_tables(pos):
    inv = ROPE_THETA ** (-jnp.arange(0, A_QK, 2, dtype=F32) / A_QK)
    ang = pos.astype(F32)[:, None] * inv[None, :]
    cos = jnp.tile(jnp.cos(ang), (1, LANES // ROPE_HALF))
    sin = jnp.tile(jnp.sin(ang), (1, LANES // ROPE_HALF))
    first_half = (jnp.arange(LANES) % (2 * ROPE_HALF)) < ROPE_HALF
    sin_lo = jnp.where(first_half[None, :], -sin, 0.0)
    sin_hi = jnp.where(first_half[None, :], 0.0, sin)
    return cos, sin_lo, sin_hi


def kernel(x_prompt, x_sample, cache_a_k, cache_a_v, state_b, state_c_ssm, state_c_conv, cache_d_k, cache_d_v,
           page_table, norm_w, mlp_w1, mlp_w2, w_in_e, w_out_e, a_lambda, a_subln, b_lower_bounds, b_norm,
           w_in_o, w_out_o, c_conv_w, c_conv_b, c_dt_bias, c_a_log, c_d, c_norm, d_sinks):
    bp, tp, _ = x_prompt.shape
    bs, ts, _ = x_sample.shape
    mp, ms = bp * tp, bs * ts
    past_len = page_table.shape[1] * PAGE_SIZE
    pos = jnp.concatenate([jnp.tile(jnp.arange(tp), bp), jnp.tile(past_len + jnp.arange(ts), bs)])
    rope_tabs = _rope_tables(pos)

    w_in_e_b = w_in_e.astype(BF16)
    dt_off = C_DINNER + C_CONV_DIM
    w_in_o_b = jnp.concatenate(
        [w_in_o[:, :, :dt_off],
         jnp.pad(w_in_o[:, :, dt_off:dt_off + C_HEADS], ((0, 0), (0, 0), (0, DT_PAD - C_HEADS))),
         w_in_o[:, :, dt_off + C_HEADS:]], axis=-1).astype(BF16)
    n_pool = cache_a_k.shape[1]
    cache_k4 = cache_a_k.reshape(cache_a_k.shape[0], n_pool, PAGE_SIZE * A_HEADS, 2 * A_QK)
    cache_v4 = cache_a_v.reshape(cache_a_v.shape[0], n_pool, PAGE_SIZE * A_HEADS, A_VD)
    w_out_e_b = w_out_e.astype(BF16)
    w_out_o_b = w_out_o.astype(BF16)
    w1_b = mlp_w1.astype(BF16)
    w2_b = mlp_w2.astype(BF16)

    even_widths = (A_QKW, A_QKW, A_W, B_KW, B_KW, B_W, B_W)
    even_rope = (True, True, False, False, False, False, False)
    odd_widths = (C_DINNER, C_CONV_DIM, DT_PAD, D_HEADS * D_HD, D_KV_HEADS * D_HD, D_KV_HEADS * D_HD)
    odd_rope = (False, False, False, True, True, False)

    x = jnp.concatenate([x_prompt.reshape(mp, D_MODEL), x_sample.reshape(ms, D_MODEL)], axis=0)
    st_p = {n: [] for n in ("a_k", "a_v", "b_state", "c_ssm", "c_conv", "d_k", "d_v")}
    st_s = {n: [] for n in st_p}
    for l in range(DEPTH):
        g = norm_w[l]
        e = l // 2
        if l % 2 == 0:
            aq, ak, av, bq, bf, bi, bg = _in_projection(x, g[0:1], w_in_e_b[e], rope_tabs, even_widths, even_rope)
            lam, lam_init = _diff_lambda(a_lambda[e], l)
            out_scale = 1.0 - lam_init
            a_p = _flash_diff_attention(aq, ak, av, bp, tp, lam, a_subln[e], out_scale)
            shp = (bs, ts, A_QKW)
            a_s = _paged_diff_attention(aq[mp:].reshape(shp), ak[mp:].reshape(shp), av[mp:].reshape(shp),
                                        cache_k4, cache_v4, page_table, e, lam, a_subln[e], out_scale)
            lb = _hgrn_lower_bound(b_lower_bounds)[e]
            b_p, bst_p = _hgrn2(bq, bf, bi, bg, lb, b_norm[e], bp, tp, 0, None, e)
            b_s, bst_s = _hgrn2(bq, bf, bi, bg, lb, b_norm[e], bs, ts, mp, state_b, e)
            mix_groups = [[a_p, b_p], [a_s.reshape(ms, A_W), b_s]]
            st_p["a_k"].append(ak[:mp].reshape(bp, tp, A_HEADS, 2 * A_QK))
            st_p["a_v"].append(av[:mp].reshape(bp, tp, A_HEADS, A_VD))
            st_p["b_state"].append(bst_p)
            st_s["a_k"].append(ak[mp:].reshape(bs, ts, A_HEADS, 2 * A_QK))
            st_s["a_v"].append(av[mp:].reshape(bs, ts, A_HEADS, A_VD))
            st_s["b_state"].append(bst_s)
            w_out = w_out_e_b[e]
        else:
            parts = _in_projection(x, g[0:1], w_in_o_b[e], rope_tabs, odd_widths, odd_rope)
            z, xbc, dt, dq, dk, dv = parts
            y_p, ssm_p, conv_p = _ssd_prompt(xbc, z, dt, c_conv_w[e], c_conv_b[e], c_dt_bias[e], c_a_log[e],
                                             c_d[e], c_norm[e], bp, tp)
            k_p = dk[:mp].reshape(bp, tp, D_KV_HEADS, D_HD)
            v_p = dv[:mp].reshape(bp, tp, D_KV_HEADS, D_HD)
            o_p = _swa_prompt_attention(dq, dk, dv, d_sinks[e], bp, tp)
            vals_p = (ssm_p, conv_p, k_p[:, -WINDOW:], v_p[:, -WINDOW:])
            y_s, ssm_s, conv_s = _ssd_sample(xbc, z, dt, state_c_conv, state_c_ssm, e, c_conv_w[e], c_conv_b[e],
                                             c_dt_bias[e], c_a_log[e], c_d[e], c_norm[e], bs, ts, mp)
            kv_w = D_KV_HEADS * D_HD
            o_s, kwin_s, vwin_s = _swa_sample_attention(dq, dk, dv, cache_d_k[e].reshape(bs, WINDOW, kv_w),
                                                        cache_d_v[e].reshape(bs, WINDOW, kv_w), d_sinks[e], bs, ts, mp)
            win_shape = (bs, WINDOW, D_KV_HEADS, D_HD)
            vals_s = (ssm_s, conv_s, kwin_s.reshape(win_shape), vwin_s.reshape(win_shape))
            mix_groups = [[y_p, o_p], [y_s, o_s]]
            for d, vals in ((st_p, vals_p), (st_s, vals_s)):
                d["c_ssm"].append(vals[0])
                d["c_conv"].append(vals[1])
                d["d_k"].append(vals[2])
                d["d_v"].append(vals[3])
            w_out = w_out_o_b[e]
        x = _post_mixer(mix_groups, x, w_out, g, w1_b[l], w2_b[l])
    sp = {n: jnp.stack(v, axis=0) for n, v in st_p.items()}
    ss = {n: jnp.stack(v, axis=0) for n, v in st_s.items()}
    y_prompt = x[:mp].reshape(bp, tp, D_MODEL)
    y_sample = x[mp:].reshape(bs, ts, D_MODEL)
    return (y_prompt, y_sample,
            sp["a_k"], ss["a_k"], sp["a_v"], ss["a_v"],
            sp["b_state"], ss["b_state"], sp["c_ssm"], ss["c_ssm"],
            sp["c_conv"], ss["c_conv"], sp["d_k"], ss["d_k"], sp["d_v"], ss["d_v"])
```

```python
import functools
import math

import jax
import jax.numpy as jnp
from jax import lax
from jax.experimental import pallas as pl
from jax.experimental.pallas import tpu as pltpu

F32 = jnp.float32
BF16 = jnp.bfloat16

D_MODEL = 1024
DEPTH = 4
PAGE_SIZE = 128
D_FF = 4 * D_MODEL
EPS = 1e-6
ROPE_THETA = 10000.0
MASK_NEG = -1e30
LB_FLOOR = 1e-30

A_HEADS = D_MODEL // 256
A_QK = 64
A_VD = 2 * A_QK
A_QKW = A_HEADS * 2 * A_QK
A_W = A_HEADS * A_VD
B_HEADS = 4
B_DK = D_MODEL // 8
B_DV = D_MODEL // 8
B_KW = B_HEADS * B_DK
B_W = B_HEADS * B_DV
C_HEADS = 16
C_HEADDIM = 64
C_DINNER = C_HEADS * C_HEADDIM
C_GROUPS = 2
C_DSTATE = 128
C_CONV = 4
C_CONV_DIM = C_DINNER + 2 * C_GROUPS * C_DSTATE
D_HEADS = 8
D_KV_HEADS = 2
D_GROUP = D_HEADS // D_KV_HEADS
D_HD = 64
WINDOW = 128

LANES = 128
ROPE_HALF = 32
DT_PAD = LANES
VMEM_LIMIT = 56 * 1024 * 1024
ROW_TILE = 512


def _rms(x, g):
    return x * lax.rsqrt(jnp.mean(x * x, axis=-1, keepdims=True) + EPS) * g


def _rope_tile(y, cos, sin_lo, sin_hi):
    up = pltpu.roll(y, LANES - ROPE_HALF, 1)
    down = pltpu.roll(y, ROPE_HALF, 1)
    return y * cos + up * sin_lo + down * sin_hi


def _proj_kernel(x_ref, g_ref, w_ref, cos_ref, slo_ref, shi_ref, *out_refs, widths, rope_flags):
    h = _rms(x_ref[...], g_ref[...]).astype(BF16)
    cos = cos_ref[...]
    slo = slo_ref[...]
    shi = shi_ref[...]
    off = 0
    for o_ref, width, roped in zip(out_refs, widths, rope_flags):
        y = jnp.dot(h, w_ref[:, off:off + width], preferred_element_type=F32)
        if roped:
            for c in range(0, width, LANES):
                o_ref[:, c:c + LANES] = _rope_tile(y[:, c:c + LANES], cos, slo, shi)
        else:
            o_ref[...] = y
        off += width


def _in_projection(x, g, w, rope_tabs, tab_map, widths, rope_flags):
    m = x.shape[0]
    n = w.shape[1]
    assert sum(widths) == n and m % ROW_TILE == 0
    row = lambda i: (i, 0)
    const = lambda i: (0, 0)
    kern = functools.partial(_proj_kernel, widths=tuple(widths), rope_flags=tuple(rope_flags))
    return pl.pallas_call(
        kern,
        grid=(m // ROW_TILE,),
        in_specs=[
            pl.BlockSpec((ROW_TILE, D_MODEL), row),
            pl.BlockSpec((1, D_MODEL), const),
            pl.BlockSpec((D_MODEL, n), const),
            pl.BlockSpec((ROW_TILE, LANES), tab_map),
            pl.BlockSpec((ROW_TILE, LANES), tab_map),
            pl.BlockSpec((ROW_TILE, LANES), tab_map),
        ],
        out_specs=[pl.BlockSpec((ROW_TILE, wd), row) for wd in widths],
        out_shape=[jax.ShapeDtypeStruct((m, wd), F32) for wd in widths],
        compiler_params=pltpu.CompilerParams(dimension_semantics=("arbitrary",), vmem_limit_bytes=VMEM_LIMIT),
        name="in_projection",
    )(x, g, w, *rope_tabs)


FF_TILE = 1024


def _post_kernel(*refs, group_sizes, group_tiles):
    n_pieces = sum(group_sizes)
    piece_refs = refs[:n_pieces]
    x_ref, wo_ref, g_ref, w1_ref, w2_ref, o_ref, x1_s, h_s, acc_s = refs[n_pieces:]
    i = pl.program_id(0)
    j = pl.program_id(1)

    def start_tile(pieces):
        m = None
        off = 0
        for p_ref in pieces:
            width = p_ref.shape[1]
            d = jnp.dot(p_ref[...].astype(BF16), wo_ref[off:off + width, :], preferred_element_type=F32)
            m = d if m is None else m + d
            off += width
        x1 = x_ref[...] + _rms(m, g_ref[1:2, :])
        x1_s[...] = x1
        h_s[...] = _rms(x1, g_ref[2:3, :]).astype(BF16)
        acc_s[...] = jnp.zeros_like(acc_s)

    first = 0
    pos = 0
    for size, tiles in zip(group_sizes, group_tiles):
        pieces = piece_refs[pos:pos + size]

        @pl.when((j == 0) & (i >= first) & (i < first + tiles))
        def _(pieces=pieces):
            start_tile(pieces)

        first += tiles
        pos += size

    a = jnp.dot(h_s[...], w1_ref[...], preferred_element_type=F32)
    a = jnp.square(jnp.maximum(a, 0.0)).astype(BF16)
    acc_s[...] += jnp.dot(a, w2_ref[...], preferred_element_type=F32)

    @pl.when(j == pl.num_programs(1) - 1)
    def _():
        o_ref[...] = x1_s[...] + _rms(acc_s[...], g_ref[3:4, :])


def _post_mixer(mix_groups, x, w_out, g, w1, w2):
    m = x.shape[0]
    kmix = w_out.shape[0]
    piece_specs = []
    pieces = []
    first = 0
    group_tiles = []
    for group in mix_groups:
        rows = group[0].shape[0]
        assert rows % ROW_TILE == 0 and sum(p.shape[1] for p in group) == kmix
        tiles = rows // ROW_TILE
        for p in group:
            assert p.shape[0] == rows
            piece_specs.append(pl.BlockSpec(
                (ROW_TILE, p.shape[1]), lambda i, j, first=first, tiles=tiles: (jnp.clip(i - first, 0, tiles - 1), 0)))
            pieces.append(p)
        group_tiles.append(tiles)
        first += tiles
    assert first * ROW_TILE == m
    kern = functools.partial(_post_kernel, group_sizes=tuple(len(grp) for grp in mix_groups),
                             group_tiles=tuple(group_tiles))
    return pl.pallas_call(
        kern,
        grid=(m // ROW_TILE, D_FF // FF_TILE),
        in_specs=piece_specs + [
            pl.BlockSpec((ROW_TILE, D_MODEL), lambda i, j: (i, 0)),
            pl.BlockSpec((kmix, D_MODEL), lambda i, j: (0, 0)),
            pl.BlockSpec((4, D_MODEL), lambda i, j: (0, 0)),
            pl.BlockSpec((D_MODEL, FF_TILE), lambda i, j: (0, j)),
            pl.BlockSpec((FF_TILE, D_MODEL), lambda i, j: (j, 0)),
        ],
        out_specs=pl.BlockSpec((ROW_TILE, D_MODEL), lambda i, j: (i, 0)),
        out_shape=jax.ShapeDtypeStruct((m, D_MODEL), F32),
        scratch_shapes=[
            pltpu.VMEM((ROW_TILE, D_MODEL), F32),
            pltpu.VMEM((ROW_TILE, D_MODEL), BF16),
            pltpu.VMEM((ROW_TILE, D_MODEL), F32),
        ],
        compiler_params=pltpu.CompilerParams(dimension_semantics=("arbitrary", "arbitrary"),
                                             vmem_limit_bytes=VMEM_LIMIT),
        name="post_mixer_mlp",
    )(*pieces, x, w_out, g, w1, w2)


DEC_PAGES = 16
A_ROWS_PER_HEAD = 2
NT_DIMS = (((1,), (1,)), ((), ()))
TN_DIMS = (((0,), (0,)), ((), ()))


def _decode_kernel(pt_ref, q_ref, kn_ref, vn_ref, sub_ref, lam_ref, *rest, n_pages, t_new, out_scale):
    del pt_ref
    k_refs = rest[:n_pages]
    v_refs = rest[n_pages:2 * n_pages]
    o_ref = rest[2 * n_pages]
    qhat_s, m_s, l_s, acc_s = rest[2 * n_pages + 1:]
    j = pl.program_id(1)
    rph = A_ROWS_PER_HEAD * t_new
    rows = A_HEADS * rph

    def page_tokens(ref):
        return jnp.concatenate([ref[pl.ds(h, PAGE_SIZE, stride=A_HEADS), :] for h in range(A_HEADS)],
                               axis=1).astype(BF16)

    def head_diag(pv):
        return jnp.concatenate([pv[h * rph:(h + 1) * rph, h * LANES:(h + 1) * LANES] for h in range(A_HEADS)], axis=0)

    @pl.when(j == 0)
    def _():
        qhat_s[...] = jnp.zeros(qhat_s.shape, BF16)
        qt = q_ref[...] * (A_QK ** -0.5)
        lane = lax.broadcasted_iota(jnp.int32, (t_new, LANES), 1)
        r = lax.broadcasted_iota(jnp.int32, (rph, PAGE_SIZE), 0)
        c = lax.broadcasted_iota(jnp.int32, (rph, PAGE_SIZE), 1)
        pad = jnp.zeros((PAGE_SIZE - t_new, LANES), F32)
        for h in range(A_HEADS):
            qh = qt[:, h * LANES:(h + 1) * LANES]
            qhat = jnp.concatenate([jnp.where(lane < A_QK, qh, 0.0), jnp.where(lane < A_QK, 0.0, qh)],
                                   axis=0).astype(BF16)
            qhat_s[h * rph:(h + 1) * rph, h * LANES:(h + 1) * LANES] = qhat
            kn = jnp.concatenate([kn_ref[:, h * LANES:(h + 1) * LANES], pad], axis=0).astype(BF16)
            vn = jnp.concatenate([vn_ref[:, h * LANES:(h + 1) * LANES], pad], axis=0).astype(BF16)
            s = lax.dot_general(qhat, kn, NT_DIMS, preferred_element_type=F32)
            s = jnp.where(c <= r % t_new, s, MASK_NEG)
            m = jnp.max(s, axis=-1, keepdims=True)
            p = jnp.exp(s - m)
            m_s[h * rph:(h + 1) * rph, :] = m
            l_s[h * rph:(h + 1) * rph, :] = jnp.sum(p, axis=-1, keepdims=True)
            acc_s[h * rph:(h + 1) * rph, :] = jnp.dot(p.astype(BF16), vn, preferred_element_type=F32)

    qhat_all = qhat_s[...]
    ss = [lax.dot_general(qhat_all, page_tokens(k_refs[n]), NT_DIMS, preferred_element_type=F32)
          for n in range(n_pages)]
    smax = functools.reduce(jnp.maximum, ss)
    m_old = m_s[...]
    m_new = jnp.maximum(m_old, jnp.max(smax, axis=-1, keepdims=True))
    alpha = jnp.exp(m_old - m_new)
    ps = [jnp.exp(s - m_new) for s in ss]
    l_s[...] = alpha * l_s[...] + jnp.sum(functools.reduce(jnp.add, ps), axis=-1, keepdims=True)
    pv = None
    for n in range(n_pages):
        d = jnp.dot(ps[n].astype(BF16), page_tokens(v_refs[n]), preferred_element_type=F32)
        pv = d if pv is None else pv + d
    acc_s[...] = alpha * acc_s[...] + head_diag(pv)
    m_s[...] = m_new

    @pl.when(j == pl.num_programs(1) - 1)
    def _():
        out = acc_s[...] / l_s[...]
        lam = lam_ref[...]
        for h in range(A_HEADS):
            o_h = out[h * rph:h * rph + t_new] - lam * out[h * rph + t_new:(h + 1) * rph]
            o_ref[:, h * LANES:(h + 1) * LANES] = _rms(o_h, sub_ref[...]) * out_scale


def _paged_diff_attention(q, k_new, v_new, cache_k, cache_v, page_table, layer_e, lam, subln, out_scale):
    b, t_new, _ = q.shape
    n_table = page_table.shape[1]
    assert n_table % DEC_PAGES == 0
    seq = lambda i, j, pt: (i, 0, 0)
    const = lambda i, j, pt: (0, 0)

    def page_spec(n):
        return pl.BlockSpec((None, None, PAGE_SIZE * A_HEADS, A_VD),
                            lambda i, j, pt: (layer_e, pt[i, j * DEC_PAGES + n], 0, 0))

    rows = A_HEADS * A_ROWS_PER_HEAD * t_new
    kern = functools.partial(_decode_kernel, n_pages=DEC_PAGES, t_new=t_new, out_scale=out_scale)
    grid_spec = pltpu.PrefetchScalarGridSpec(
        num_scalar_prefetch=1,
        grid=(b, n_table // DEC_PAGES),
        in_specs=[pl.BlockSpec((None, t_new, A_QKW), seq),
                  pl.BlockSpec((None, t_new, A_QKW), seq),
                  pl.BlockSpec((None, t_new, A_W), seq),
                  pl.BlockSpec((1, A_VD), const),
                  pl.BlockSpec((1, A_VD), const)]
                 + [page_spec(n) for n in range(DEC_PAGES)] * 2,
        out_specs=pl.BlockSpec((None, t_new, A_W), seq),
        scratch_shapes=[pltpu.VMEM((rows, A_QKW), BF16),
                        pltpu.VMEM((rows, 1), F32),
                        pltpu.VMEM((rows, 1), F32),
                        pltpu.VMEM((rows, A_VD), F32)],
    )
    lam_row = jnp.full((1, A_VD), lam, F32)
    return pl.pallas_call(
        kern,
        grid_spec=grid_spec,
        out_shape=jax.ShapeDtypeStruct((b, t_new, A_W), F32),
        compiler_params=pltpu.CompilerParams(dimension_semantics=("arbitrary", "arbitrary"),
                                             vmem_limit_bytes=VMEM_LIMIT),
        name="paged_diff_attention",
    )(page_table, q, k_new, v_new, subln.reshape(1, A_VD), lam_row,
      *([cache_k] * DEC_PAGES), *([cache_v] * DEC_PAGES))


ATT_TILE = 512


def _flash_kernel(q_ref, k_ref, v_ref, sub_ref, lam_ref, o_ref, qhat_s, m_s, l_s, acc_s, *, out_scale):
    qi = pl.program_id(2)
    ki = pl.program_id(3)
    tq = q_ref.shape[0]
    tk = k_ref.shape[0]

    @pl.when(ki == 0)
    def _():
        qt = q_ref[...] * (A_QK ** -0.5)
        lane = lax.broadcasted_iota(jnp.int32, (tq, LANES), 1)
        qhat_s[0:tq, :] = jnp.where(lane < A_QK, qt, 0.0).astype(BF16)
        qhat_s[tq:2 * tq, :] = jnp.where(lane < A_QK, 0.0, qt).astype(BF16)
        m_s[...] = jnp.full(m_s.shape, MASK_NEG, F32)
        l_s[...] = jnp.zeros(l_s.shape, F32)
        acc_s[...] = jnp.zeros(acc_s.shape, F32)

    def step(on_diagonal):
        s = lax.dot_general(qhat_s[...], k_ref[...].astype(BF16), NT_DIMS, preferred_element_type=F32)
        if on_diagonal:
            r = lax.broadcasted_iota(jnp.int32, (2 * tq, tk), 0)
            c = lax.broadcasted_iota(jnp.int32, (2 * tq, tk), 1)
            s = jnp.where(c <= jnp.where(r >= tq, r - tq, r), s, MASK_NEG)
        m_old = m_s[...]
        m_new = jnp.maximum(m_old, jnp.max(s, axis=-1, keepdims=True))
        alpha = jnp.exp(m_old - m_new)
        p = jnp.exp(s - m_new)
        l_s[...] = alpha * l_s[...] + jnp.sum(p, axis=-1, keepdims=True)
        acc_s[...] = alpha * acc_s[...] + jnp.dot(p.astype(BF16), v_ref[...].astype(BF16),
                                                  preferred_element_type=F32)
        m_s[...] = m_new

    @pl.when(ki < qi)
    def _():
        step(False)

    @pl.when(ki == qi)
    def _():
        step(True)
        out = acc_s[...] / l_s[...]
        o = out[0:tq] - lam_ref[...] * out[tq:2 * tq]
        o_ref[...] = _rms(o, sub_ref[...]) * out_scale


def _flash_diff_attention(q, k, v, n_batch, seq_len, lam, subln, out_scale):
    assert seq_len % ATT_TILE == 0
    nt = seq_len // ATT_TILE
    qmap = lambda b, h, qi, ki: (b * nt + qi, h)
    kmap = lambda b, h, qi, ki: (b * nt + jnp.minimum(ki, qi), h)
    const = lambda b, h, qi, ki: (0, 0)
    kern = functools.partial(_flash_kernel, out_scale=out_scale)
    return pl.pallas_call(
        kern,
        grid=(n_batch, A_HEADS, nt, nt),
        in_specs=[pl.BlockSpec((ATT_TILE, LANES), qmap),
                  pl.BlockSpec((ATT_TILE, LANES), kmap),
                  pl.BlockSpec((ATT_TILE, LANES), kmap),
                  pl.BlockSpec((1, A_VD), const),
                  pl.BlockSpec((1, A_VD), const)],
        out_specs=pl.BlockSpec((ATT_TILE, LANES), qmap),
        out_shape=jax.ShapeDtypeStruct((n_batch * seq_len, A_W), F32),
        scratch_shapes=[pltpu.VMEM((2 * ATT_TILE, LANES), BF16),
                        pltpu.VMEM((2 * ATT_TILE, 1), F32),
                        pltpu.VMEM((2 * ATT_TILE, 1), F32),
                        pltpu.VMEM((2 * ATT_TILE, A_VD), F32)],
        compiler_params=pltpu.CompilerParams(dimension_semantics=("arbitrary",) * 4, vmem_limit_bytes=VMEM_LIMIT),
        name="flash_diff_attention",
    )(q, k, v, subln.reshape(1, A_VD), jnp.full((1, A_VD), lam, F32))


SUBLANES = 8
HGRN_CHUNK = 32
HGRN_BLOCK = 512


def _split3_dot(tri, x):
    hi = x.astype(BF16)
    r1 = x - hi.astype(F32)
    mid = r1.astype(BF16)
    lo = (r1 - mid.astype(F32)).astype(BF16)
    return (jnp.dot(tri, hi, preferred_element_type=F32) + jnp.dot(tri, mid, preferred_element_type=F32)
            + jnp.dot(tri, lo, preferred_element_type=F32))


def _cumsum_rows(x, tri):
    n = x.shape[0]
    if n > SUBLANES:
        return _split3_dot(tri, x)
    row = lax.broadcasted_iota(jnp.int32, x.shape, 0)
    shift = 1
    while shift < n:
        x = x + jnp.where(row >= shift, pltpu.roll(x, shift, 0), 0.0)
        shift *= 2
    return x


def _hgrn2_kernel(*refs, chunk, has_s0, n_seq):
    if has_s0:
        q_ref, f_ref, i_ref, g_ref, lb_ref, nw_ref, s0_ref, o_ref, sout_ref, st_s = refs
    else:
        q_ref, f_ref, i_ref, g_ref, lb_ref, nw_ref, o_ref, sout_ref, st_s = refs
        s0_ref = None
    tb = pl.program_id(1)
    n_chunks = q_ref.shape[0] // chunk
    n_tiles = chunk // SUBLANES

    def load_state(i):
        for h in range(B_HEADS):
            st_s[h] = s0_ref[i, h].T if has_s0 else jnp.zeros((B_DV, B_DK), F32)

    def store_state(i):
        for h in range(B_HEADS):
            sout_ref[i, h] = st_s[h].T

    nw = nw_ref[...]
    tri = (lax.broadcasted_iota(jnp.int32, (chunk, chunk), 0)
           >= lax.broadcasted_iota(jnp.int32, (chunk, chunk), 1)).astype(BF16)
    row8 = lax.broadcasted_iota(jnp.int32, (SUBLANES, 1), 0)
    mxu = (lambda a: a.astype(BF16)) if chunk >= 2 * SUBLANES else (lambda a: a)

    def head_chunk(r0, h):
        cols = slice(h * B_DK, (h + 1) * B_DK)
        lb = lb_ref[:, cols]
        log_lb = jnp.log(jnp.maximum(lb, LB_FLOOR))
        z = f_ref[pl.ds(r0, chunk), cols]
        qq = q_ref[pl.ds(r0, chunk), cols]
        vv = i_ref[pl.ds(r0, chunk), cols]
        gg = g_ref[pl.ds(r0, chunk), cols]
        log_sig = jnp.minimum(z, 0.0) - jnp.log1p(jnp.exp(-jnp.abs(z)))
        arg = jnp.log1p(-lb) + log_sig
        log_f = jnp.maximum(log_lb, arg) + jnp.log1p(jnp.exp(-jnp.abs(log_lb - arg)))
        kk = (1.0 - lb) * jax.nn.sigmoid(-z)
        qf = qq * jax.nn.sigmoid(qq)
        gcum = _cumsum_rows(log_f, tri)
        g_last = gcum[chunk - 1:chunk, :]
        st = st_s[h]
        o_inter = lax.dot_general(mxu(qf * jnp.exp(gcum)), mxu(st), NT_DIMS, preferred_element_type=F32)
        g_t = [gcum[i * SUBLANES:(i + 1) * SUBLANES] for i in range(n_tiles)]
        q_t = [qf[i * SUBLANES:(i + 1) * SUBLANES] for i in range(n_tiles)]
        o_t = [None] * n_tiles
        for s in range(chunk):
            g_s = gcum[s:s + 1, :]
            k_s = kk[s:s + 1, :]
            v_s = vv[s:s + 1, :]
            for ti in range(s // SUBLANES, n_tiles):
                e = jnp.exp(jnp.minimum(g_t[ti] - g_s, 0.0))
                col = jnp.sum(q_t[ti] * (k_s * e), axis=-1, keepdims=True)
                if ti == s // SUBLANES:
                    col = jnp.where(row8 >= s % SUBLANES, col, 0.0)
                term = col * v_s
                o_t[ti] = term if o_t[ti] is None else o_t[ti] + term
        ob = o_inter + jnp.concatenate(o_t, axis=0)
        kw = kk * jnp.exp(g_last - gcum)
        st_s[h] = st * jnp.exp(g_last) + lax.dot_general(mxu(vv), mxu(kw), TN_DIMS, preferred_element_type=F32)
        o_ref[pl.ds(r0, chunk), cols] = _rms(ob, nw) * jax.nn.sigmoid(gg)

    def body(c, carry):
        r0 = c * chunk if isinstance(c, int) else pl.multiple_of(c * chunk, chunk)
        for h in range(B_HEADS):
            head_chunk(r0, h)
        return carry

    if n_seq == 1:
        @pl.when(tb == 0)
        def _():
            load_state(0)

        lax.fori_loop(0, n_chunks, body, 0)

        @pl.when(tb == pl.num_programs(1) - 1)
        def _():
            store_state(0)
    else:
        for i in range(n_seq):
            load_state(i)
            body(i, 0)
            store_state(i)


HGRN_SEQS_PER_STEP = 8


def _hgrn2(bq, bf, bi, bg, lb, b_norm_w, n_batch, seq_len, row_off, s0, layer_e):
    if seq_len <= HGRN_CHUNK:
        n_seq = math.gcd(n_batch, HGRN_SEQS_PER_STEP)
        chunk = seq_len
        blk = n_seq * seq_len
        nb = 1
    else:
        n_seq = 1
        blk = min(HGRN_BLOCK, seq_len)
        chunk = math.gcd(blk, HGRN_CHUNK)
        nb = seq_len // blk
    assert seq_len % chunk == 0 and chunk % SUBLANES == 0 and row_off % blk == 0
    off = row_off // blk
    rmap = lambda b, t: (off + b * nb + t, 0)
    omap = lambda b, t: (b * nb + t, 0)
    const = lambda b, t: (0, 0)
    in_specs = [pl.BlockSpec((blk, B_KW), rmap)] * 4 + [pl.BlockSpec((1, B_KW), const), pl.BlockSpec((1, B_DV), const)]
    args = [bq, bf, bi, bg, lb.reshape(1, B_KW), b_norm_w.reshape(1, B_DV)]
    if s0 is not None:
        in_specs.append(pl.BlockSpec((None, n_seq, B_HEADS, B_DK, B_DV), lambda b, t: (layer_e, b, 0, 0, 0)))
        args.append(s0)
    kern = functools.partial(_hgrn2_kernel, chunk=chunk, has_s0=s0 is not None, n_seq=n_seq)
    return pl.pallas_call(
        kern,
        grid=(n_batch // n_seq, nb),
        in_specs=in_specs,
        out_specs=[pl.BlockSpec((blk, B_W), omap),
                   pl.BlockSpec((n_seq, B_HEADS, B_DK, B_DV), lambda b, t: (b, 0, 0, 0))],
        out_shape=[jax.ShapeDtypeStruct((n_batch * seq_len, B_W), F32),
                   jax.ShapeDtypeStruct((n_batch, B_HEADS, B_DK, B_DV), F32)],
        scratch_shapes=[pltpu.VMEM((B_HEADS, B_DV, B_DK), F32)],
        compiler_params=pltpu.CompilerParams(dimension_semantics=("arbitrary",) * 2, vmem_limit_bytes=VMEM_LIMIT),
        name="hgrn2",
    )(*args)


SSD_CHUNK = 128
C_PAIRS = C_HEADS // 2
C_GROUP_W = C_DINNER // C_GROUPS


def _softplus(x):
    return jnp.maximum(x, 0.0) + jnp.log1p(jnp.exp(-jnp.abs(x)))


def _ssd_kernel(xbc_ref, z_ref, dt_ref, cw_ref, cb_ref, dtb_ref, alog_ref, cd_ref, nw_ref,
                y_ref, sout_ref, cout_ref, xpad_s, st_s):
    c = pl.program_id(1)
    L = xbc_ref.shape[0]
    tail = C_CONV - 1

    @pl.when(c == 0)
    def _():
        xpad_s[0:SUBLANES, :] = jnp.zeros((SUBLANES, C_CONV_DIM), F32)
        st_s[...] = jnp.zeros(st_s.shape, F32)

    xpad_s[SUBLANES:SUBLANES + L, :] = xbc_ref[...]
    acc = cb_ref[...] + cw_ref[0:1, :] * xpad_s[pl.ds(SUBLANES - tail, L), :]
    for j in range(1, C_CONV):
        acc = acc + cw_ref[j:j + 1, :] * xpad_s[pl.ds(SUBLANES - tail + j, L), :]
    xc = acc * jax.nn.sigmoid(acc)

    @pl.when(c == pl.num_programs(1) - 1)
    def _():
        cout_ref[...] = xpad_s[pl.ds(SUBLANES + L - tail, tail), :]

    xpad_s[0:SUBLANES, :] = xpad_s[L:L + SUBLANES, :]

    dtv = _softplus(dt_ref[...] + dtb_ref[...])
    a = dtv * (-jnp.exp(alog_ref[...]))
    row = lax.broadcasted_iota(jnp.int32, (L, L), 0)
    col = lax.broadcasted_iota(jnp.int32, (L, L), 1)
    causal = row >= col
    gcum = _split3_dot(causal.astype(BF16), a)
    gcum_t = gcum.T
    lane = lax.broadcasted_iota(jnp.int32, (L, LANES), 1)
    low = lane < C_HEADDIM
    srow = lax.broadcasted_iota(jnp.int32, (2 * C_HEADDIM, C_DSTATE), 0)

    for g in range(C_GROUPS):
        b_g = xc[:, C_DINNER + g * C_DSTATE:C_DINNER + (g + 1) * C_DSTATE].astype(BF16)
        c_g = xc[:, C_DINNER + (C_GROUPS + g) * C_DSTATE:C_DINNER + (C_GROUPS + g + 1) * C_DSTATE].astype(BF16)
        cb = lax.dot_general(c_g, b_g, NT_DIMS, preferred_element_type=F32)
        u_parts = []
        sq = None
        for jp in range(C_PAIRS // C_GROUPS):
            pair = g * (C_PAIRS // C_GROUPS) + jp
            x_pair = xc[:, pair * LANES:(pair + 1) * LANES]
            y_pair = None
            gb = []
            dtb = []
            for hh in range(2):
                h = 2 * pair + hh
                g_col = jnp.broadcast_to(gcum[:, h:h + 1], (L, LANES))
                dt_col = jnp.broadcast_to(dtv[:, h:h + 1], (L, LANES))
                gb.append(g_col)
                dtb.append(dt_col)
                dec = jnp.exp(jnp.minimum(g_col - gcum_t[h:h + 1, :], 0.0))
                m_h = jnp.where(causal, cb * dec, 0.0).astype(BF16)
                half = low if hh == 0 else jnp.logical_not(low)
                xdt = jnp.where(half, x_pair * dt_col, 0.0).astype(BF16)
                d = jnp.dot(m_h, xdt, preferred_element_type=F32)
                y_pair = d if y_pair is None else y_pair + d
            st = st_s[pair]
            y_inter = lax.dot_general(c_g, st.astype(BF16), NT_DIMS, preferred_element_type=F32)
            y_pair = y_pair + y_inter * jnp.where(low, jnp.exp(gb[0]), jnp.exp(gb[1]))
            y_pair = y_pair + cd_ref[:, pair * LANES:(pair + 1) * LANES] * x_pair
            gl = [gb[0][L - 1:L, :], gb[1][L - 1:L, :]]
            w_pair = jnp.where(low, jnp.exp(gl[0] - gb[0]) * dtb[0], jnp.exp(gl[1] - gb[1]) * dtb[1])
            decay = jnp.where(srow < C_HEADDIM, jnp.exp(gl[0]), jnp.exp(gl[1]))
            st_s[pair] = decay * st + lax.dot_general((x_pair * w_pair).astype(BF16), b_g, TN_DIMS,
                                                      preferred_element_type=F32)
            zz = z_ref[:, pair * LANES:(pair + 1) * LANES]
            u = y_pair * (zz * jax.nn.sigmoid(zz))
            u_parts.append(u)
            s2 = jnp.sum(u * u, axis=-1, keepdims=True)
            sq = s2 if sq is None else sq + s2
        inv = lax.rsqrt(sq * (1.0 / C_GROUP_W) + EPS)
        for jp, u in enumerate(u_parts):
            pair = g * (C_PAIRS // C_GROUPS) + jp
            y_ref[:, pair * LANES:(pair + 1) * LANES] = u * inv * nw_ref[:, pair * LANES:(pair + 1) * LANES]

    @pl.when(c == pl.num_programs(1) - 1)
    def _():
        sout_ref[...] = st_s[...]


def _ssd_prompt(xbc, z, dt, conv_w, conv_b, dt_bias, a_log, c_d, c_norm_w, n_batch, seq_len):
    assert seq_len % SSD_CHUNK == 0
    nc = seq_len // SSD_CHUNK
    rmap = lambda b, c: (b * nc + c, 0)
    const = lambda b, c: (0, 0)
    pad16 = lambda v: jnp.pad(v.astype(F32), (0, DT_PAD - C_HEADS)).reshape(1, DT_PAD)
    y, s_out, c_out = pl.pallas_call(
        _ssd_kernel,
        grid=(n_batch, nc),
        in_specs=[pl.BlockSpec((SSD_CHUNK, C_CONV_DIM), rmap),
                  pl.BlockSpec((SSD_CHUNK, C_DINNER), rmap),
                  pl.BlockSpec((SSD_CHUNK, DT_PAD), rmap),
                  pl.BlockSpec((C_CONV, C_CONV_DIM), const),
                  pl.BlockSpec((1, C_CONV_DIM), const),
                  pl.BlockSpec((1, DT_PAD), const),
                  pl.BlockSpec((1, DT_PAD), const),
                  pl.BlockSpec((1, C_DINNER), const),
                  pl.BlockSpec((1, C_DINNER), const)],
        out_specs=[pl.BlockSpec((SSD_CHUNK, C_DINNER), rmap),
                   pl.BlockSpec((None, C_PAIRS, 2 * C_HEADDIM, C_DSTATE), lambda b, c: (b, 0, 0, 0)),
                   pl.BlockSpec((None, C_CONV - 1, C_CONV_DIM), lambda b, c: (b, 0, 0))],
        out_shape=[jax.ShapeDtypeStruct((n_batch * seq_len, C_DINNER), F32),
                   jax.ShapeDtypeStruct((n_batch, C_PAIRS, 2 * C_HEADDIM, C_DSTATE), F32),
                   jax.ShapeDtypeStruct((n_batch, C_CONV - 1, C_CONV_DIM), F32)],
        scratch_shapes=[pltpu.VMEM((SUBLANES + SSD_CHUNK, C_CONV_DIM), F32),
                        pltpu.VMEM((C_PAIRS, 2 * C_HEADDIM, C_DSTATE), F32)],
        compiler_params=pltpu.CompilerParams(dimension_semantics=("arbitrary", "arbitrary"),
                                             vmem_limit_bytes=VMEM_LIMIT),
        name="ssd_prompt",
    )(xbc, z, dt, conv_w, conv_b.reshape(1, C_CONV_DIM), pad16(dt_bias), pad16(a_log),
      jnp.repeat(c_d.astype(F32), C_HEADDIM).reshape(1, C_DINNER), c_norm_w.reshape(1, C_DINNER))
    return y, s_out.reshape(n_batch, C_HEADS, C_HEADDIM, C_DSTATE), c_out


def _swa_kernel(q_ref, kp_ref, kc_ref, vp_ref, vc_ref, sink_ref, o_ref):
    i = pl.program_id(1)
    w = q_ref.shape[0]
    kk = jnp.concatenate([kp_ref[...], kc_ref[...]], axis=0).astype(BF16)
    vv = jnp.concatenate([vp_ref[...], vc_ref[...]], axis=0).astype(BF16)
    rows = D_GROUP * w
    r = lax.broadcasted_iota(jnp.int32, (rows, 2 * w), 0) % w
    c = lax.broadcasted_iota(jnp.int32, (rows, 2 * w), 1)
    first_key = jnp.where(i > 0, 0, w)
    visible = (c <= w + r) & (c > r) & (c >= first_key)
    lane = lax.broadcasted_iota(jnp.int32, (w, LANES), 1)
    low = lane < D_HD
    for g in range(D_KV_HEADS):
        g_half = low if g == 0 else jnp.logical_not(low)
        q_parts = []
        sink_parts = []
        for rr in range(D_GROUP):
            h = g * D_GROUP + rr
            pair, hh = divmod(h, 2)
            q_pair = q_ref[:, pair * LANES:(pair + 1) * LANES] * (D_HD ** -0.5)
            q_src = q_pair if hh == g else pltpu.roll(q_pair, D_HD, 1)
            q_parts.append(jnp.where(g_half, q_src, 0.0).astype(BF16))
            sink_parts.append(jnp.broadcast_to(sink_ref[h:h + 1, 0:1], (w, 1)))
        qg = jnp.concatenate(q_parts, axis=0)
        sink = jnp.concatenate(sink_parts, axis=0)
        s = lax.dot_general(qg, kk, NT_DIMS, preferred_element_type=F32)
        s = jnp.where(visible, s, MASK_NEG)
        m = jnp.maximum(jnp.max(s, axis=-1, keepdims=True), sink)
        p = jnp.exp(s - m)
        p = p / (jnp.sum(p, axis=-1, keepdims=True) + jnp.exp(sink - m))
        o = jnp.dot(p.astype(BF16), vv, preferred_element_type=F32)
        for jp in range(D_GROUP // 2):
            pair = g * (D_GROUP // 2) + jp
            halves = []
            for hh in range(2):
                o_h = o[(2 * jp + hh) * w:(2 * jp + hh + 1) * w]
                halves.append(o_h if hh == g else pltpu.roll(o_h, D_HD, 1))
            o_ref[:, pair * LANES:(pair + 1) * LANES] = jnp.where(low, halves[0], halves[1])


def _swa_prompt_attention(dq, dk, dv, sinks, n_batch, seq_len):
    assert seq_len % WINDOW == 0
    nb = seq_len // WINDOW
    cur = lambda b, i: (b * nb + i, 0)
    prev = lambda b, i: (b * nb + jnp.maximum(i - 1, 0), 0)
    kv_w = D_KV_HEADS * D_HD
    return pl.pallas_call(
        _swa_kernel,
        grid=(n_batch, nb),
        in_specs=[pl.BlockSpec((WINDOW, D_HEADS * D_HD), cur),
                  pl.BlockSpec((WINDOW, kv_w), prev),
                  pl.BlockSpec((WINDOW, kv_w), cur),
                  pl.BlockSpec((WINDOW, kv_w), prev),
                  pl.BlockSpec((WINDOW, kv_w), cur),
                  pl.BlockSpec((D_HEADS, LANES), lambda b, i: (0, 0))],
        out_specs=pl.BlockSpec((WINDOW, D_HEADS * D_HD), cur),
        out_shape=jax.ShapeDtypeStruct((n_batch * seq_len, D_HEADS * D_HD), F32),
        compiler_params=pltpu.CompilerParams(dimension_semantics=("arbitrary", "arbitrary"),
                                             vmem_limit_bytes=VMEM_LIMIT),
        name="swa_prompt",
    )(dq, dk, dk, dv, dv, jnp.broadcast_to(sinks.astype(F32)[:, None], (D_HEADS, LANES)))


SAMPLE_SSD_SEQS = 16
SAMPLE_SWA_SEQS = 8


def _ssd_sample_kernel(xbc_ref, z_ref, dt_ref, c0_ref, s0_ref, cw_ref, cb_ref, dtb_ref, alog_ref, cd_ref, nw_ref,
                       y_ref, sout_ref, cout_ref, xpad_s, xc_s, *, t_len):
    L = xbc_ref.shape[0]
    n_seq = L // t_len
    tail = C_CONV - 1
    for i in range(n_seq):
        xpad_s[pl.ds(SUBLANES - tail, tail), :] = c0_ref[i]
        xpad_s[SUBLANES:SUBLANES + t_len, :] = xbc_ref[i * t_len:(i + 1) * t_len, :]
        acc = cb_ref[...] + cw_ref[0:1, :] * xpad_s[pl.ds(SUBLANES - tail, t_len), :]
        for j in range(1, C_CONV):
            acc = acc + cw_ref[j:j + 1, :] * xpad_s[pl.ds(SUBLANES - tail + j, t_len), :]
        xc_s[i * t_len:(i + 1) * t_len, :] = acc * jax.nn.sigmoid(acc)
        cout_ref[i] = xpad_s[pl.ds(SUBLANES + t_len - tail, tail), :]
    xc = xc_s[...]

    dtv = _softplus(dt_ref[...] + dtb_ref[...])
    a = dtv * (-jnp.exp(alog_ref[...]))
    row = lax.broadcasted_iota(jnp.int32, (L, L), 0)
    col = lax.broadcasted_iota(jnp.int32, (L, L), 1)
    same = (row // t_len) == (col // t_len)
    causal = (row >= col) & same
    gcum = _split3_dot(causal.astype(BF16), a)
    gtot = _split3_dot(same.astype(BF16), a)
    gcum_t = gcum.T
    lane = lax.broadcasted_iota(jnp.int32, (L, LANES), 1)
    low = lane < C_HEADDIM
    srow = lax.broadcasted_iota(jnp.int32, (2 * C_HEADDIM, C_DSTATE), 0)

    for g in range(C_GROUPS):
        b_f = xc[:, C_DINNER + g * C_DSTATE:C_DINNER + (g + 1) * C_DSTATE]
        c_f = xc[:, C_DINNER + (C_GROUPS + g) * C_DSTATE:C_DINNER + (C_GROUPS + g + 1) * C_DSTATE]
        b_g = b_f.astype(BF16)
        c_g = c_f.astype(BF16)
        cb = lax.dot_general(c_g, b_g, NT_DIMS, preferred_element_type=F32)
        u_parts = []
        sq = None
        for jp in range(C_PAIRS // C_GROUPS):
            pair = g * (C_PAIRS // C_GROUPS) + jp
            x_pair = xc[:, pair * LANES:(pair + 1) * LANES]
            y_pair = None
            gb, tb_, dtb = [], [], []
            for hh in range(2):
                h = 2 * pair + hh
                g_col = jnp.broadcast_to(gcum[:, h:h + 1], (L, LANES))
                dt_col = jnp.broadcast_to(dtv[:, h:h + 1], (L, LANES))
                gb.append(g_col)
                tb_.append(jnp.broadcast_to(gtot[:, h:h + 1], (L, LANES)))
                dtb.append(dt_col)
                dec = jnp.exp(jnp.minimum(g_col - gcum_t[h:h + 1, :], 0.0))
                m_h = jnp.where(causal, cb * dec, 0.0).astype(BF16)
                half = low if hh == 0 else jnp.logical_not(low)
                xdt = jnp.where(half, x_pair * dt_col, 0.0).astype(BF16)
                d = jnp.dot(m_h, xdt, preferred_element_type=F32)
                y_pair = d if y_pair is None else y_pair + d
            xw = x_pair * jnp.where(low, jnp.exp(tb_[0] - gb[0]) * dtb[0], jnp.exp(tb_[1] - gb[1]) * dtb[1])
            y_inter = []
            for i in range(n_seq):
                rows = slice(i * t_len, (i + 1) * t_len)
                st = s0_ref[i, pair]
                y_inter.append(lax.dot_general(c_f[rows], st, NT_DIMS, preferred_element_type=F32))
                decay = jnp.where(srow < C_HEADDIM, jnp.exp(tb_[0][i * t_len:i * t_len + 1, :]),
                                  jnp.exp(tb_[1][i * t_len:i * t_len + 1, :]))
                sout_ref[i, pair] = decay * st + lax.dot_general(xw[rows], b_f[rows], TN_DIMS,
                                                                 preferred_element_type=F32)
            y_pair = y_pair + jnp.concatenate(y_inter, axis=0) * jnp.where(low, jnp.exp(gb[0]), jnp.exp(gb[1]))
            y_pair = y_pair + cd_ref[:, pair * LANES:(pair + 1) * LANES] * x_pair
            zz = z_ref[:, pair * LANES:(pair + 1) * LANES]
            u = y_pair * (zz * jax.nn.sigmoid(zz))
            u_parts.append(u)
            s2 = jnp.sum(u * u, axis=-1, keepdims=True)
            sq = s2 if sq is None else sq + s2
        inv = lax.rsqrt(sq * (1.0 / C_GROUP_W) + EPS)
        for jp, u in enumerate(u_parts):
            pair = g * (C_PAIRS // C_GROUPS) + jp
            y_ref[:, pair * LANES:(pair + 1) * LANES] = u * inv * nw_ref[:, pair * LANES:(pair + 1) * LANES]


def _ssd_sample(xbc, z, dt, conv0, s0, layer_e, conv_w, conv_b, dt_bias, a_log, c_d, c_norm_w, n_batch, t_len, row_off):
    blk = SAMPLE_SSD_SEQS * t_len
    assert n_batch % SAMPLE_SSD_SEQS == 0 and row_off % blk == 0 and t_len == SUBLANES and blk == SSD_CHUNK
    off = row_off // blk
    rmap = lambda b: (off + b, 0)
    omap = lambda b: (b, 0)
    const = lambda b: (0, 0)
    pad16 = lambda v: jnp.pad(v.astype(F32), (0, DT_PAD - C_HEADS)).reshape(1, DT_PAD)
    s0_pairs = s0.reshape(s0.shape[0], n_batch, C_PAIRS, 2 * C_HEADDIM, C_DSTATE)
    kern = functools.partial(_ssd_sample_kernel, t_len=t_len)
    y, s_out, c_out = pl.pallas_call(
        kern,
        grid=(n_batch // SAMPLE_SSD_SEQS,),
        in_specs=[pl.BlockSpec((blk, C_CONV_DIM), rmap),
                  pl.BlockSpec((blk, C_DINNER), rmap),
                  pl.BlockSpec((blk, DT_PAD), rmap),
                  pl.BlockSpec((None, SAMPLE_SSD_SEQS, C_CONV - 1, C_CONV_DIM), lambda b: (layer_e, b, 0, 0)),
                  pl.BlockSpec((None, SAMPLE_SSD_SEQS, C_PAIRS, 2 * C_HEADDIM, C_DSTATE),
                               lambda b: (layer_e, b, 0, 0, 0)),
                  pl.BlockSpec((C_CONV, C_CONV_DIM), const),
                  pl.BlockSpec((1, C_CONV_DIM), const),
                  pl.BlockSpec((1, DT_PAD), const),
                  pl.BlockSpec((1, DT_PAD), const),
                  pl.BlockSpec((1, C_DINNER), const),
                  pl.BlockSpec((1, C_DINNER), const)],
        out_specs=[pl.BlockSpec((blk, C_DINNER), omap),
                   pl.BlockSpec((SAMPLE_SSD_SEQS, C_PAIRS, 2 * C_HEADDIM, C_DSTATE), lambda b: (b, 0, 0, 0)),
                   pl.BlockSpec((SAMPLE_SSD_SEQS, C_CONV - 1, C_CONV_DIM), lambda b: (b, 0, 0))],
        out_shape=[jax.ShapeDtypeStruct((n_batch * t_len, C_DINNER), F32),
                   jax.ShapeDtypeStruct((n_batch, C_PAIRS, 2 * C_HEADDIM, C_DSTATE), F32),
                   jax.ShapeDtypeStruct((n_batch, C_CONV - 1, C_CONV_DIM), F32)],
        scratch_shapes=[pltpu.VMEM((2 * SUBLANES, C_CONV_DIM), F32),
                        pltpu.VMEM((blk, C_CONV_DIM), F32)],
        compiler_params=pltpu.CompilerParams(dimension_semantics=("arbitrary",), vmem_limit_bytes=VMEM_LIMIT),
        name="ssd_sample",
    )(xbc, z, dt, conv0, s0_pairs, conv_w, conv_b.reshape(1, C_CONV_DIM), pad16(dt_bias), pad16(a_log),
      jnp.repeat(c_d.astype(F32), C_HEADDIM).reshape(1, C_DINNER), c_norm_w.reshape(1, C_DINNER))
    return y, s_out.reshape(n_batch, C_HEADS, C_HEADDIM, C_DSTATE), c_out


def _swa_sample_kernel(q_ref, kn_ref, vn_ref, kc_ref, vc_ref, sink_ref, o_ref, kw_ref, vw_ref, *, t_len):
    n_seq = q_ref.shape[0] // t_len
    rows = D_GROUP * t_len
    r = lax.broadcasted_iota(jnp.int32, (rows, WINDOW), 0) % t_len
    c = lax.broadcasted_iota(jnp.int32, (rows, WINDOW), 1)
    cache_visible = c > r
    new_visible = c <= r
    lane = lax.broadcasted_iota(jnp.int32, (t_len, LANES), 1)
    low = lane < D_HD
    pad = jnp.zeros((WINDOW - t_len, LANES), F32)
    for i in range(n_seq):
        tok = slice(i * t_len, (i + 1) * t_len)
        kc = kc_ref[i]
        vc = vc_ref[i]
        kn = kn_ref[tok, :]
        vn = vn_ref[tok, :]
        kw_ref[i, 0:WINDOW - t_len, :] = kc[t_len:WINDOW]
        kw_ref[i, WINDOW - t_len:WINDOW, :] = kn
        vw_ref[i, 0:WINDOW - t_len, :] = vc[t_len:WINDOW]
        vw_ref[i, WINDOW - t_len:WINDOW, :] = vn
        kc_b = kc.astype(BF16)
        vc_b = vc.astype(BF16)
        kn_b = jnp.concatenate([kn, pad], axis=0).astype(BF16)
        vn_b = jnp.concatenate([vn, pad], axis=0).astype(BF16)
        for g in range(D_KV_HEADS):
            g_half = low if g == 0 else jnp.logical_not(low)
            q_parts = []
            sink_parts = []
            for rr in range(D_GROUP):
                h = g * D_GROUP + rr
                pair, hh = divmod(h, 2)
                q_pair = q_ref[tok, pair * LANES:(pair + 1) * LANES] * (D_HD ** -0.5)
                q_src = q_pair if hh == g else pltpu.roll(q_pair, D_HD, 1)
                q_parts.append(jnp.where(g_half, q_src, 0.0))
                sink_parts.append(jnp.broadcast_to(sink_ref[h:h + 1, 0:1], (t_len, 1)))
            qg = jnp.concatenate(q_parts, axis=0).astype(BF16)
            sink = jnp.concatenate(sink_parts, axis=0)
            s_c = jnp.where(cache_visible, lax.dot_general(qg, kc_b, NT_DIMS, preferred_element_type=F32), MASK_NEG)
            s_n = jnp.where(new_visible, lax.dot_general(qg, kn_b, NT_DIMS, preferred_element_type=F32), MASK_NEG)
            m = jnp.maximum(jnp.max(jnp.maximum(s_c, s_n), axis=-1, keepdims=True), sink)
            p_c = jnp.exp(s_c - m)
            p_n = jnp.exp(s_n - m)
            denom = jnp.sum(p_c + p_n, axis=-1, keepdims=True) + jnp.exp(sink - m)
            o = (jnp.dot(p_c.astype(BF16), vc_b, preferred_element_type=F32)
                 + jnp.dot(p_n.astype(BF16), vn_b, preferred_element_type=F32)) / denom
            for jp in range(D_GROUP // 2):
                pair = g * (D_GROUP // 2) + jp
                halves = []
                for hh in range(2):
                    o_h = o[(2 * jp + hh) * t_len:(2 * jp + hh + 1) * t_len]
                    halves.append(o_h if hh == g else pltpu.roll(o_h, D_HD, 1))
                o_ref[tok, pair * LANES:(pair + 1) * LANES] = jnp.where(low, halves[0], halves[1])


def _swa_sample_attention(dq, dk, dv, cache_k, cache_v, sinks, n_batch, t_len, row_off):
    blk = SAMPLE_SWA_SEQS * t_len
    assert n_batch % SAMPLE_SWA_SEQS == 0 and row_off % blk == 0 and t_len == SUBLANES
    off = row_off // blk
    rmap = lambda b: (off + b, 0)
    kv_w = D_KV_HEADS * D_HD
    win = pl.BlockSpec((SAMPLE_SWA_SEQS, WINDOW, kv_w), lambda b: (b, 0, 0))
    kern = functools.partial(_swa_sample_kernel, t_len=t_len)
    return pl.pallas_call(
        kern,
        grid=(n_batch // SAMPLE_SWA_SEQS,),
        in_specs=[pl.BlockSpec((blk, D_HEADS * D_HD), rmap),
                  pl.BlockSpec((blk, kv_w), rmap),
                  pl.BlockSpec((blk, kv_w), rmap),
                  win, win,
                  pl.BlockSpec((D_HEADS, LANES), lambda b: (0, 0))],
        out_specs=[pl.BlockSpec((blk, D_HEADS * D_HD), lambda b: (b, 0)), win, win],
        out_shape=[jax.ShapeDtypeStruct((n_batch * t_len, D_HEADS * D_HD), F32),
                   jax.ShapeDtypeStruct((n_batch, WINDOW, kv_w), F32),
                   jax.ShapeDtypeStruct((n_batch, WINDOW, kv_w), F32)],
        compiler_params=pltpu.CompilerParams(dimension_semantics=("arbitrary",), vmem_limit_bytes=VMEM_LIMIT),
        name="swa_sample",
    )(dq, dk, dv, cache_k, cache_v, jnp.broadcast_to(sinks.astype(F32)[:, None], (D_HEADS, LANES)))


def _diff_lambda(lam, layer):
    lam = lam.astype(F32)
    lam_init = 0.8 - 0.6 * math.exp(-0.3 * layer)
    lam_full = jnp.exp(jnp.sum(lam[0] * lam[1])) - jnp.exp(jnp.sum(lam[2] * lam[3])) + lam_init
    return lam_full, lam_init


def _hgrn_lower_bound(p):
    sm = jax.nn.softmax(p.astype(F32), axis=0)
    return jnp.concatenate([jnp.zeros_like(sm[:1]), jnp.cumsum(sm[1:], axis=0)], axis=0)


def _rope_tables(pos):
    inv = ROPE_THETA ** (-jnp.arange(0, A_QK, 2, dtype=F32) / A_QK)
    ang = pos.astype(F32)[:, None] * inv[None, :]
    cos = jnp.tile(jnp.cos(ang), (1, LANES // ROPE_HALF))
    sin = jnp.tile(jnp.sin(ang), (1, LANES // ROPE_HALF))
    first_half = (jnp.arange(LANES) % (2 * ROPE_HALF)) < ROPE_HALF
    sin_lo = jnp.where(first_half[None, :], -sin, 0.0)
    sin_hi = jnp.where(first_half[None, :], 0.0, sin)
    return cos, sin_lo, sin_hi


def kernel(x_prompt, x_sample, cache_a_k, cache_a_v, state_b, state_c_ssm, state_c_conv, cache_d_k, cache_d_v,
           page_table, norm_w, mlp_w1, mlp_w2, w_in_e, w_out_e, a_lambda, a_subln, b_lower_bounds, b_norm,
           w_in_o, w_out_o, c_conv_w, c_conv_b, c_dt_bias, c_a_log, c_d, c_norm, d_sinks):
    bp, tp, _ = x_prompt.shape
    bs, ts, _ = x_sample.shape
    mp, ms = bp * tp, bs * ts
    past_len = page_table.shape[1] * PAGE_SIZE
    assert tp % ROW_TILE == 0 and ROW_TILE % ts == 0 and ms % ROW_TILE == 0
    pos = jnp.concatenate([jnp.arange(tp), jnp.tile(past_len + jnp.arange(ts), ROW_TILE // ts)])
    rope_tabs = _rope_tables(pos)
    prompt_tiles, seq_tiles = mp // ROW_TILE, tp // ROW_TILE
    tab_map = lambda i: (jnp.where(i < prompt_tiles, i % seq_tiles, seq_tiles), 0)

    w_in_e_b = w_in_e.astype(BF16)
    dt_off = C_DINNER + C_CONV_DIM
    w_in_o_b = jnp.concatenate(
        [w_in_o[:, :, :dt_off],
         jnp.pad(w_in_o[:, :, dt_off:dt_off + C_HEADS], ((0, 0), (0, 0), (0, DT_PAD - C_HEADS))),
         w_in_o[:, :, dt_off + C_HEADS:]], axis=-1).astype(BF16)
    n_pool = cache_a_k.shape[1]
    cache_k4 = cache_a_k.reshape(cache_a_k.shape[0], n_pool, PAGE_SIZE * A_HEADS, 2 * A_QK)
    cache_v4 = cache_a_v.reshape(cache_a_v.shape[0], n_pool, PAGE_SIZE * A_HEADS, A_VD)
    w_out_e_b = w_out_e.astype(BF16)
    w_out_o_b = w_out_o.astype(BF16)
    w1_b = mlp_w1.astype(BF16)
    w2_b = mlp_w2.astype(BF16)

    even_widths = (A_QKW, A_QKW, A_W, B_KW, B_KW, B_W, B_W)
    even_rope = (True, True, False, False, False, False, False)
    odd_widths = (C_DINNER, C_CONV_DIM, DT_PAD, D_HEADS * D_HD, D_KV_HEADS * D_HD, D_KV_HEADS * D_HD)
    odd_rope = (False, False, False, True, True, False)

    x = jnp.concatenate([x_prompt.reshape(mp, D_MODEL), x_sample.reshape(ms, D_MODEL)], axis=0)
    st_p = {n: [] for n in ("a_k", "a_v", "b_state", "c_ssm", "c_conv", "d_k", "d_v")}
    st_s = {n: [] for n in st_p}
    for l in range(DEPTH):
        g = norm_w[l]
        e = l // 2
        if l % 2 == 0:
            aq, ak, av, bq, bf, bi, bg = _in_projection(x, g[0:1], w_in_e_b[e], rope_tabs, tab_map, even_widths, even_rope)
            lam, lam_init = _diff_lambda(a_lambda[e], l)
            out_scale = 1.0 - lam_init
            a_p = _flash_diff_attention(aq, ak, av, bp, tp, lam, a_subln[e], out_scale)
            shp = (bs, ts, A_QKW)
            a_s = _paged_diff_attention(aq[mp:].reshape(shp), ak[mp:].reshape(shp), av[mp:].reshape(shp),
                                        cache_k4, cache_v4, page_table, e, lam, a_subln[e], out_scale)
            lb = _hgrn_lower_bound(b_lower_bounds)[e]
            b_p, bst_p = _hgrn2(bq, bf, bi, bg, lb, b_norm[e], bp, tp, 0, None, e)
            b_s, bst_s = _hgrn2(bq, bf, bi, bg, lb, b_norm[e], bs, ts, mp, state_b, e)
            mix_groups = [[a_p, b_p], [a_s.reshape(ms, A_W), b_s]]
            st_p["a_k"].append(ak[:mp].reshape(bp, tp, A_HEADS, 2 * A_QK))
            st_p["a_v"].append(av[:mp].reshape(bp, tp, A_HEADS, A_VD))
            st_p["b_state"].append(bst_p)
            st_s["a_k"].append(ak[mp:].reshape(bs, ts, A_HEADS, 2 * A_QK))
            st_s["a_v"].append(av[mp:].reshape(bs, ts, A_HEADS, A_VD))
            st_s["b_state"].append(bst_s)
            w_out = w_out_e_b[e]
        else:
            parts = _in_projection(x, g[0:1], w_in_o_b[e], rope_tabs, tab_map, odd_widths, odd_rope)
            z, xbc, dt, dq, dk, dv = parts
            y_p, ssm_p, conv_p = _ssd_prompt(xbc, z, dt, c_conv_w[e], c_conv_b[e], c_dt_bias[e], c_a_log[e],
                                             c_d[e], c_norm[e], bp, tp)
            k_p = dk[:mp].reshape(bp, tp, D_KV_HEADS, D_HD)
            v_p = dv[:mp].reshape(bp, tp, D_KV_HEADS, D_HD)
            o_p = _swa_prompt_attention(dq, dk, dv, d_sinks[e], bp, tp)
            vals_p = (ssm_p, conv_p, k_p[:, -WINDOW:], v_p[:, -WINDOW:])
            y_s, ssm_s, conv_s = _ssd_sample(xbc, z, dt, state_c_conv, state_c_ssm, e, c_conv_w[e], c_conv_b[e],
                                             c_dt_bias[e], c_a_log[e], c_d[e], c_norm[e], bs, ts, mp)
            kv_w = D_KV_HEADS * D_HD
            o_s, kwin_s, vwin_s = _swa_sample_attention(dq, dk, dv, cache_d_k[e].reshape(bs, WINDOW, kv_w),
                                                        cache_d_v[e].reshape(bs, WINDOW, kv_w), d_sinks[e], bs, ts, mp)
            win_shape = (bs, WINDOW, D_KV_HEADS, D_HD)
            vals_s = (ssm_s, conv_s, kwin_s.reshape(win_shape), vwin_s.reshape(win_shape))
            mix_groups = [[y_p, o_p], [y_s, o_s]]
            for d, vals in ((st_p, vals_p), (st_s, vals_s)):
                d["c_ssm"].append(vals[0])
                d["c_conv"].append(vals[1])
                d["d_k"].append(vals[2])
                d["d_v"].append(vals[3])
            w_out = w_out_o_b[e]
        x = _post_mixer(mix_groups, x, w_out, g, w1_b[l], w2_b[l])
    sp = {n: jnp.stack(v, axis=0) for n, v in st_p.items()}
    ss = {n: jnp.stack(v, axis=0) for n, v in st_s.items()}
    y_prompt = x[:mp].reshape(bp, tp, D_MODEL)
    y_sample = x[mp:].reshape(bs, ts, D_MODEL)
    return (y_prompt, y_sample,
            sp["a_k"], ss["a_k"], sp["a_v"], ss["a_v"],
            sp["b_state"], ss["b_state"], sp["c_ssm"], ss["c_ssm"],
            sp["c_conv"], ss["c_conv"], sp["d_k"], ss["d_k"], sp["d_v"], ss["d_v"])
```
